```python
import jax
import jax.numpy as jnp
from jax import lax
import numpy as np


D_MODEL = 1024
BATCH = 8
SEQ = 8192
DEPTH = 4

GRID_W = 64
CTX_LEN = 256
N_MIXERS = 2
N_CONV_LAYERS = (DEPTH + 1) // 2
N_ATTN_LAYERS = DEPTH // 2
CONV_WIDTH = 31
HEAD_DIM = 64
N_HEADS = D_MODEL // HEAD_DIM
N_KV_HEADS = N_HEADS // 4
GQA_GROUP = N_HEADS // N_KV_HEADS
QKV_DIM = (N_HEADS + 2 * N_KV_HEADS) * HEAD_DIM
WINDOW = 128
BLOCK = 128
ROPE_BASE = 10000.0
D_FF = 2816
FFN_CONV_WIDTH = 3
EPS = 1e-6
NEG = -1e30

kernel_name = 'hybrid_conformer_swa_convffn_dit'


def rmsnorm(x, g):
    xf = x.astype(jnp.float32)
    y = xf * lax.rsqrt(jnp.mean(xf * xf, axis=-1, keepdims=True) + EPS)
    return (y * g.astype(jnp.float32)).astype(x.dtype)


def layernorm(x, g, b):
    xf = x.astype(jnp.float32)
    mu = jnp.mean(xf, axis=-1, keepdims=True)
    var = jnp.mean(jnp.square(xf - mu), axis=-1, keepdims=True)
    y = (xf - mu) * lax.rsqrt(var + EPS)
    return (y * g.astype(jnp.float32) + b.astype(jnp.float32)).astype(x.dtype)


def modulate(x, shift, scale):
    return x * (1.0 + scale) + shift


def dwconv(x, w, b):
    k = w.shape[0]
    pad = (k - 1) // 2
    y = lax.conv_general_dilated(x, w[:, None, :], window_strides=(1,), padding=[(pad, pad)],
                                 dimension_numbers=('NWC', 'WIO', 'NWC'),
                                 feature_group_count=x.shape[-1])
    return y + b


def axial_rope_tables(seq_len):
    rows = seq_len // GRID_W
    row = jnp.broadcast_to(jnp.arange(rows)[:, None], (rows, GRID_W)).reshape(-1).astype(jnp.float32)
    col = jnp.broadcast_to(jnp.arange(GRID_W)[None, :], (rows, GRID_W)).reshape(-1).astype(jnp.float32)
    n_freq = HEAD_DIM // 4
    inv = ROPE_BASE ** (-jnp.arange(n_freq, dtype=jnp.float32) / n_freq)
    ang = jnp.concatenate([row[:, None] * inv, col[:, None] * inv], axis=-1)
    return jnp.cos(ang), jnp.sin(ang)


def apply_rope(x, cos, sin):
    xf = x.astype(jnp.float32)
    half = HEAD_DIM // 2
    x1, x2 = xf[..., :half], xf[..., half:]
    c = cos[None, :, None, :]
    s = sin[None, :, None, :]
    return jnp.concatenate([x1 * c - x2 * s, x2 * c + x1 * s], axis=-1).astype(x.dtype)


def conformer_conv(h, w_pw1, b_pw1, w_dw, b_dw, ln_g, ln_b, w_pw2, b_pw2):
    u = h @ w_pw1 + b_pw1
    a, g = jnp.split(u, 2, axis=-1)
    v = a * jax.nn.sigmoid(g)
    v = dwconv(v, w_dw, b_dw)
    v = jax.nn.silu(layernorm(v, ln_g, ln_b))
    return v @ w_pw2 + b_pw2


def conv_ffn(h, w_up, w_dw, b_dw, w_down):
    u = dwconv(h @ w_up, w_dw, b_dw)
    a, b = jnp.split(u, 2, axis=-1)
    return (jax.nn.silu(a) * b) @ w_down


def split_qkv(h, w_qkv):
    bsz, length, _ = h.shape
    qkv = h @ w_qkv
    q = qkv[..., :N_HEADS * HEAD_DIM].reshape(bsz, length, N_HEADS, HEAD_DIM)
    k = qkv[..., N_HEADS * HEAD_DIM:(N_HEADS + N_KV_HEADS) * HEAD_DIM].reshape(bsz, length, N_KV_HEADS, HEAD_DIM)
    v = qkv[..., (N_HEADS + N_KV_HEADS) * HEAD_DIM:].reshape(bsz, length, N_KV_HEADS, HEAD_DIM)
    return q, k, v


def sink_logits(sink, bsz, nq):
    s = sink.astype(jnp.float32).reshape(N_KV_HEADS, GQA_GROUP)[None, :, :, None, None]
    return jnp.broadcast_to(s, (bsz, N_KV_HEADS, GQA_GROUP, nq, 1))


def window_attention(h_lat, h_ctx, w_qkv, w_o, sink, cos, sin, with_ctx_out):
    bsz, seq, _ = h_lat.shape
    scale = HEAD_DIM ** -0.5
    qc, kc, vc = split_qkv(h_ctx, w_qkv)
    q, k, v = split_qkv(h_lat, w_qkv)
    q = apply_rope(q, cos, sin) * scale
    k = apply_rope(k, cos, sin)
    q = q.reshape(bsz, seq, N_KV_HEADS, GQA_GROUP, HEAD_DIM)
    nb = seq // BLOCK
    k_pad = jnp.pad(k, ((0, 0), (BLOCK, BLOCK), (0, 0), (0, 0)))
    v_pad = jnp.pad(v, ((0, 0), (BLOCK, BLOCK), (0, 0), (0, 0)))
    q_blocks = jnp.moveaxis(q.reshape(bsz, nb, BLOCK, N_KV_HEADS, GQA_GROUP, HEAD_DIM), 1, 0)
    s_sink_lat = sink_logits(sink, bsz, BLOCK)

    def block_fn(args):
        i, qb = args
        kb = lax.dynamic_slice_in_dim(k_pad, i * BLOCK, 3 * BLOCK, axis=1)
        vb = lax.dynamic_slice_in_dim(v_pad, i * BLOCK, 3 * BLOCK, axis=1)
        qpos = i * BLOCK + jnp.arange(BLOCK)
        kpos = i * BLOCK - BLOCK + jnp.arange(3 * BLOCK)
        valid = (jnp.abs(qpos[:, None] - kpos[None, :]) <= WINDOW) & (kpos[None, :] >= 0) & (kpos[None, :] < seq)
        s_band = jnp.einsum('bqhgd,bkhd->bhgqk', qb, kb).astype(jnp.float32)
        s_band = jnp.where(valid[None, None, None], s_band, NEG)
        s_ctx = jnp.einsum('bqhgd,bkhd->bhgqk', qb, kc).astype(jnp.float32)
        p = jax.nn.softmax(jnp.concatenate([s_band, s_ctx, s_sink_lat], axis=-1), axis=-1)
        p_band = p[..., :3 * BLOCK].astype(vb.dtype)
        p_ctx = p[..., 3 * BLOCK:3 * BLOCK + CTX_LEN].astype(vc.dtype)
        o = jnp.einsum('bhgqk,bkhd->bqhgd', p_band, vb) + jnp.einsum('bhgqk,bkhd->bqhgd', p_ctx, vc)
        return o

    o_blocks = lax.map(block_fn, (jnp.arange(nb), q_blocks))
    o_lat = jnp.moveaxis(o_blocks, 0, 1).reshape(bsz, seq, N_HEADS * HEAD_DIM) @ w_o

    o_ctx = None
    if with_ctx_out:
        qcs = (qc * scale).reshape(bsz, CTX_LEN, N_KV_HEADS, GQA_GROUP, HEAD_DIM)
        s_c = jnp.einsum('bqhgd,bkhd->bhgqk', qcs, kc).astype(jnp.float32)
        p_c = jax.nn.softmax(jnp.concatenate([s_c, sink_logits(sink, bsz, CTX_LEN)], axis=-1), axis=-1)
        oc = jnp.einsum('bhgqk,bkhd->bqhgd', p_c[..., :CTX_LEN].astype(vc.dtype), vc)
        o_ctx = oc.reshape(bsz, CTX_LEN, N_HEADS * HEAD_DIM) @ w_o
    return o_lat, o_ctx


def setup_inputs(seed: int = 0) -> dict:
    key = jax.random.key(seed)
    ks = jax.random.split(key, 32)
    D = D_MODEL

    def nrm(k, shape, scale):
        return jax.random.normal(k, shape, jnp.float32) * scale

    return {
        'x': nrm(ks[0], (BATCH, SEQ, D), 1.0),
        'c': nrm(ks[1], (BATCH, D), 1.0),
        'ctx': nrm(ks[2], (BATCH, CTX_LEN, D), 1.0),
        'c_ctx': nrm(ks[3], (D,), 1.0),
        'ada_w': nrm(ks[4], (DEPTH, D, 6 * D), 0.02),
        'ada_b': nrm(ks[5], (DEPTH, 6 * D), 0.02),
        'norm1_g': 1.0 + nrm(ks[6], (DEPTH, D), 0.05),
        'norm2_g': 1.0 + nrm(ks[7], (DEPTH, D), 0.05),
        'final_g': 1.0 + nrm(ks[8], (D,), 0.05),
        'conv_w_pw1': nrm(ks[9], (N_CONV_LAYERS, D, 2 * D), D ** -0.5),
        'conv_b_pw1': nrm(ks[10], (N_CONV_LAYERS, 2 * D), 0.02),
        'conv_w_dw': nrm(ks[11], (N_CONV_LAYERS, CONV_WIDTH, D), CONV_WIDTH ** -0.5),
        'conv_b_dw': nrm(ks[12], (N_CONV_LAYERS, D), 0.02),
        'conv_ln_g': 1.0 + nrm(ks[13], (N_CONV_LAYERS, D), 0.05),
        'conv_ln_b': nrm(ks[14], (N_CONV_LAYERS, D), 0.02),
        'conv_w_pw2': nrm(ks[15], (N_CONV_LAYERS, D, D), D ** -0.5),
        'conv_b_pw2': nrm(ks[16], (N_CONV_LAYERS, D), 0.02),
        'attn_w_qkv': nrm(ks[17], (N_ATTN_LAYERS, D, QKV_DIM), D ** -0.5),
        'attn_w_o': nrm(ks[18], (N_ATTN_LAYERS, N_HEADS * HEAD_DIM, D), (N_HEADS * HEAD_DIM) ** -0.5),
        'attn_sink': nrm(ks[19], (N_ATTN_LAYERS, N_HEADS), 1.0),
        'ffn_w_up': nrm(ks[20], (DEPTH, D, 2 * D_FF), D ** -0.5),
        'ffn_w_dw': nrm(ks[21], (DEPTH, FFN_CONV_WIDTH, 2 * D_FF), FFN_CONV_WIDTH ** -0.5),
        'ffn_b_dw': nrm(ks[22], (DEPTH, 2 * D_FF), 0.02),
        'ffn_w_down': nrm(ks[23], (DEPTH, D_FF, D), D_FF ** -0.5),
    }


def reference(x, c, ctx, c_ctx, ada_w, ada_b, norm1_g, norm2_g, final_g,
              conv_w_pw1, conv_b_pw1, conv_w_dw, conv_b_dw, conv_ln_g, conv_ln_b, conv_w_pw2, conv_b_pw2,
              attn_w_qkv, attn_w_o, attn_sink,
              ffn_w_up, ffn_w_dw, ffn_b_dw, ffn_w_down):
    cos, sin = axial_rope_tables(x.shape[1])
    sc = jax.nn.silu(c)
    scc = jax.nn.silu(c_ctx)
    h_lat = x
    h_ctx = ctx
    for i in range(DEPTH):
        last = i == DEPTH - 1
        mod_lat = (sc @ ada_w[i] + ada_b[i])[:, None, :]
        mod_ctx = scc @ ada_w[i] + ada_b[i]
        sh1, sc1, g1, sh2, sc2, g2 = jnp.split(mod_lat, 6, axis=-1)
        csh1, csc1, cg1, csh2, csc2, cg2 = jnp.split(mod_ctx, 6, axis=-1)
        a_lat = modulate(rmsnorm(h_lat, norm1_g[i]), sh1, sc1)
        j = i // N_MIXERS
        if i % N_MIXERS == 0:
            y_lat = conformer_conv(a_lat, conv_w_pw1[j], conv_b_pw1[j], conv_w_dw[j], conv_b_dw[j],
                                   conv_ln_g[j], conv_ln_b[j], conv_w_pw2[j], conv_b_pw2[j])
            if not last:
                a_ctx = modulate(rmsnorm(h_ctx, norm1_g[i]), csh1, csc1)
                y_ctx = conformer_conv(a_ctx, conv_w_pw1[j], conv_b_pw1[j], conv_w_dw[j], conv_b_dw[j],
                                       conv_ln_g[j], conv_ln_b[j], conv_w_pw2[j], conv_b_pw2[j])
        else:
            a_ctx = modulate(rmsnorm(h_ctx, norm1_g[i]), csh1, csc1)
            y_lat, y_ctx = window_attention(a_lat, a_ctx, attn_w_qkv[j], attn_w_o[j], attn_sink[j],
                                            cos, sin, not last)
        h_lat = h_lat + g1 * y_lat
        b_lat = modulate(rmsnorm(h_lat, norm2_g[i]), sh2, sc2)
        h_lat = h_lat + g2 * conv_ffn(b_lat, ffn_w_up[i], ffn_w_dw[i], ffn_b_dw[i], ffn_w_down[i])
        if not last:
            h_ctx = h_ctx + cg1 * y_ctx
            b_ctx = modulate(rmsnorm(h_ctx, norm2_g[i]), csh2, csc2)
            h_ctx = h_ctx + cg2 * conv_ffn(b_ctx, ffn_w_up[i], ffn_w_dw[i], ffn_b_dw[i], ffn_w_down[i])
    return rmsnorm(h_lat, final_g)
```

```python
import functools

import jax
import jax.numpy as jnp
from jax import lax
from jax.experimental import pallas as pl
from jax.experimental.pallas import tpu as pltpu

F32 = jnp.float32
BF16 = jnp.bfloat16

EPS = 1e-6
NEG = -1e30
HEAD_DIM = 64
N_HEADS = 16
N_KV_HEADS = 4
WINDOW = 128
BLOCK = 128
GRID_W = 64
ROPE_BASE = 10000.0

LANES = 128
HALO = 16
VMEM_LIMIT = 56 * 1024 * 1024


def _cparams(n_axes):
    return pltpu.CompilerParams(
        dimension_semantics=("arbitrary",) * n_axes,
        vmem_limit_bytes=VMEM_LIMIT,
    )


def _const_spec(shape):
    nd = len(shape)
    return pl.BlockSpec(shape, lambda *_: (0,) * nd, pipeline_mode=pl.Buffered(1))


def _rms_mod(x, g, shift, scale):
    ms = jnp.mean(x * x, axis=-1, keepdims=True)
    y = x * lax.rsqrt(ms + EPS) * g
    return y * (1.0 + scale) + shift


def _rms(x, g):
    ms = jnp.mean(x * x, axis=-1, keepdims=True)
    return x * lax.rsqrt(ms + EPS) * g


def _mod_kernel(c_ref, w_ref, b_ref, o_ref):
    c = c_ref[...]
    sc = c * jax.nn.sigmoid(c)
    o_ref[0] = jnp.dot(sc, w_ref[0], preferred_element_type=F32,
                       precision=lax.Precision.HIGHEST) + b_ref[0]


def _modulation(c_all, ada_w, ada_b):
    depth, d, n = ada_w.shape
    rows = c_all.shape[0]
    nb = 1536
    return pl.pallas_call(
        _mod_kernel,
        out_shape=jax.ShapeDtypeStruct((depth, rows, n), F32),
        grid=(depth, n // nb),
        in_specs=[
            pl.BlockSpec((rows, d), lambda i, j: (0, 0)),
            pl.BlockSpec((1, d, nb), lambda i, j: (i, 0, j)),
            pl.BlockSpec((1, 1, nb), lambda i, j: (i, 0, j)),
        ],
        out_specs=pl.BlockSpec((1, rows, nb), lambda i, j: (i, 0, j)),
        compiler_params=_cparams(2),
        name="adaln_mod",
    )(c_all, ada_w, ada_b.reshape(depth, 1, n))


def _conv_front_kernel(h_ref, mod_ref, ng_ref, w_ref, b_ref, v_ref):
    x = h_ref[0]
    d = x.shape[-1]
    a = _rms_mod(x, ng_ref[...], mod_ref[0, 0:1], mod_ref[0, 1:2])
    u = jnp.dot(a.astype(BF16), w_ref[...], preferred_element_type=F32) + b_ref[...]
    v_ref[0] = u[:, :d] * jax.nn.sigmoid(u[:, d:])


def _conv_front(h, mod, ng, w_pw1, b_pw1, tm):
    s, l, d = h.shape
    return pl.pallas_call(
        _conv_front_kernel,
        out_shape=jax.ShapeDtypeStruct((s, l, d), F32),
        grid=(s, l // tm),
        in_specs=[
            pl.BlockSpec((1, tm, d), lambda i, t: (i, t, 0)),
            pl.BlockSpec((1, 6, d), lambda i, t: (i, 0, 0)),
            _const_spec((1, d)),
            _const_spec((d, 2 * d)),
            _const_spec((1, 2 * d)),
        ],
        out_specs=pl.BlockSpec((1, tm, d), lambda i, t: (i, t, 0)),
        compiler_params=_cparams(2),
        name="conv_front",
    )(h, mod, ng, w_pw1, b_pw1)


def _halo_specs(tm, width, n_tiles):
    r = tm // HALO

    def prev_map(i, t):
        return (i, jnp.maximum(t * r - 1, 0), 0)

    def next_map(i, t):
        return (i, jnp.minimum((t + 1) * r, n_tiles * r - 1), 0)

    return [
        pl.BlockSpec((1, HALO, width), prev_map),
        pl.BlockSpec((1, tm, width), lambda i, t: (i, t, 0)),
        pl.BlockSpec((1, HALO, width), next_map),
    ]


def _conv_back_kernel(vp_ref, v_ref, vn_ref, h_ref, mod_ref, wdw_ref, bdw_ref, lng_ref, lnb_ref,
                      w2_ref, b2_ref, n2g_ref, hout_ref, b_ref, vext_ref, conv_ref, *, n_taps):
    t = pl.program_id(1)
    nt = pl.num_programs(1)
    tm = v_ref.shape[1]
    d = v_ref.shape[2]
    ncb = d // LANES
    rc = 64

    vp = jnp.where(t > 0, vp_ref[0], 0.0)
    vn = jnp.where(t < nt - 1, vn_ref[0], 0.0)
    for cb in range(ncb):
        cs = slice(cb * LANES, (cb + 1) * LANES)
        vext_ref[cb, 0:HALO, :] = vp[:, cs]
        vext_ref[cb, HALO:HALO + tm, :] = v_ref[0, :, cs]
        vext_ref[cb, HALO + tm:, :] = vn[:, cs]

    base = HALO - (n_taps - 1) // 2

    def col_block(cb, carry):
        bias = bdw_ref[cb]
        for rb in range(tm // rc):
            acc = jnp.zeros((rc, LANES), F32) + bias
            for k in range(n_taps):
                off = rb * rc + base + k
                acc = acc + vext_ref[cb, off:off + rc, :] * wdw_ref[cb, k:k + 1, :]
            conv_ref[cb, rb * rc:(rb + 1) * rc, :] = acc
        return carry

    lax.fori_loop(0, ncb, col_block, 0)

    y = jnp.concatenate([conv_ref[cb] for cb in range(ncb)], axis=1)
    mu = jnp.mean(y, axis=-1, keepdims=True)
    yc = y - mu
    var = jnp.mean(yc * yc, axis=-1, keepdims=True)
    z = yc * lax.rsqrt(var + EPS) * lng_ref[...] + lnb_ref[...]
    z = z * jax.nn.sigmoid(z)
    o = jnp.dot(z.astype(BF16), w2_ref[...], preferred_element_type=F32) + b2_ref[...]
    hn = h_ref[0] + mod_ref[0, 2:3] * o
    hout_ref[0] = hn
    b_ref[0] = _rms_mod(hn, n2g_ref[...], mod_ref[0, 3:4], mod_ref[0, 4:5]).astype(BF16)


def _conv_back(v, h, mod, w_dw, b_dw, ln_g, ln_b, w_pw2, b_pw2, n2g, tm):
    s, l, d = h.shape
    n_taps = w_dw.shape[0]
    ncb = d // LANES
    taps_pad = -(-n_taps // 8) * 8
    wdw = jnp.pad(w_dw, ((0, taps_pad - n_taps), (0, 0))).reshape(taps_pad, ncb, LANES).transpose(1, 0, 2)
    bdw = b_dw.reshape(ncb, 1, LANES)
    tile = pl.BlockSpec((1, tm, d), lambda i, t: (i, t, 0))
    return pl.pallas_call(
        functools.partial(_conv_back_kernel, n_taps=n_taps),
        out_shape=(jax.ShapeDtypeStruct((s, l, d), F32), jax.ShapeDtypeStruct((s, l, d), BF16)),
        grid=(s, l // tm),
        in_specs=_halo_specs(tm, d, l // tm) + [
            tile,
            pl.BlockSpec((1, 6, d), lambda i, t: (i, 0, 0)),
            _const_spec((ncb, taps_pad, LANES)),
            _const_spec((ncb, 1, LANES)),
            _const_spec((1, d)),
            _const_spec((1, d)),
            _const_spec((d, d)),
            _const_spec((1, d)),
            _const_spec((1, d)),
        ],
        out_specs=(tile, tile),
        scratch_shapes=[
            pltpu.VMEM((ncb, tm + 2 * HALO, LANES), F32),
            pltpu.VMEM((ncb, tm, LANES), F32),
        ],
        compiler_params=_cparams(2),
        name="conv_back",
    )(v, v, v, h, mod, wdw, bdw, ln_g, ln_b, w_pw2, b_pw2, n2g)


def _ffn_kernel(xp_ref, x_ref, xn_ref, h_ref, mod_ref, wup_ref, wdw_ref, bdw_ref, wdn_ref, fg_ref,
                out_ref, xs_ref, ua_ref, ub_ref, acc_ref, *, final_norm):
    t = pl.program_id(1)
    nt = pl.num_programs(1)
    tm = x_ref.shape[1]
    n_chunks = wdn_ref.shape[0]

    xs_ref[0:HALO, :] = jnp.where(t > 0, xp_ref[0], jnp.zeros_like(xp_ref[0]))
    xs_ref[HALO:HALO + tm, :] = x_ref[0]
    xs_ref[HALO + tm:, :] = jnp.where(t < nt - 1, xn_ref[0], jnp.zeros_like(xn_ref[0]))
    acc_ref[...] = jnp.zeros_like(acc_ref)

    def chunk(c, carry):
        xs = xs_ref[...]
        ua_ref[...] = jnp.dot(xs, wup_ref[0, c], preferred_element_type=F32)
        ub_ref[...] = jnp.dot(xs, wup_ref[1, c], preferred_element_type=F32)

        def dw(u_ref, half):
            w = wdw_ref[half, c]
            return (u_ref[HALO - 1:HALO - 1 + tm, :] * w[0:1]
                    + u_ref[HALO:HALO + tm, :] * w[1:2]
                    + u_ref[HALO + 1:HALO + 1 + tm, :] * w[2:3]
                    + bdw_ref[half, c])

        ca = dw(ua_ref, 0)
        cb = dw(ub_ref, 1)
        y = (ca * jax.nn.sigmoid(ca)) * cb
        acc_ref[...] += jnp.dot(y.astype(BF16), wdn_ref[c], preferred_element_type=F32)
        return carry

    lax.fori_loop(0, n_chunks, chunk, 0)

    hn = h_ref[0] + mod_ref[0, 5:6] * acc_ref[...]
    if final_norm:
        hn = _rms(hn, fg_ref[...])
    out_ref[0] = hn


def _ffn(b, h, mod, wup, wdw, bdw, wdn, final_g, tm, final_norm):
    s, l, d = h.shape
    n_chunks, cf, _ = wdn.shape
    tile = pl.BlockSpec((1, tm, d), lambda i, t: (i, t, 0))
    return pl.pallas_call(
        functools.partial(_ffn_kernel, final_norm=final_norm),
        out_shape=jax.ShapeDtypeStruct((s, l, d), F32),
        grid=(s, l // tm),
        in_specs=_halo_specs(tm, d, l // tm) + [
            tile,
            pl.BlockSpec((1, 6, d), lambda i, t: (i, 0, 0)),
            _const_spec(wup.shape),
            _const_spec(wdw.shape),
            _const_spec(bdw.shape),
            _const_spec(wdn.shape),
            _const_spec((1, d)),
        ],
        out_specs=tile,
        scratch_shapes=[
            pltpu.VMEM((tm + 2 * HALO, d), BF16),
            pltpu.VMEM((tm + 2 * HALO, cf), F32),
            pltpu.VMEM((tm + 2 * HALO, cf), F32),
            pltpu.VMEM((tm, d), F32),
        ],
        compiler_params=_cparams(2),
        name="conv_ffn",
    )(b, b, b, h, mod, wup, wdw, bdw, wdn, final_g)


def _swap_halves(x):
    lane = lax.broadcasted_iota(jnp.int32, x.shape, 1)
    return jnp.where((lane & (HEAD_DIM - 1)) < HEAD_DIM // 2,
                     pltpu.roll(x, LANES - HEAD_DIM // 2, 1),
                     pltpu.roll(x, HEAD_DIM // 2, 1))


def _qkv_kernel(h_ref, mod_ref, ng_ref, w_ref, cos_ref, sin_ref, q_ref, k_ref, v_ref, *, rope):
    x = h_ref[0]
    a = _rms_mod(x, ng_ref[...], mod_ref[0, 0:1], mod_ref[0, 1:2])
    qkv = jnp.dot(a.astype(BF16), w_ref[...], preferred_element_type=F32)
    nq = q_ref.shape[2]
    nk = k_ref.shape[2]
    scale = HEAD_DIM ** -0.5
    if rope:
        cos = cos_ref[...]
        sin = sin_ref[...]
        blocks = []
        for j in range((nq + nk) // LANES):
            blk = qkv[:, j * LANES:(j + 1) * LANES]
            blocks.append(blk * cos + _swap_halves(blk) * sin)
        for j in range(nq // LANES):
            q_ref[0, :, j * LANES:(j + 1) * LANES] = (blocks[j] * scale).astype(BF16)
        for j in range(nk // LANES):
            k_ref[0, :, j * LANES:(j + 1) * LANES] = blocks[nq // LANES + j].astype(BF16)
    else:
        q_ref[0] = (qkv[:, :nq] * scale).astype(BF16)
        k_ref[0] = qkv[:, nq:nq + nk].astype(BF16)
    v_ref[0] = qkv[:, nq + nk:].astype(BF16)


def _qkv(h, mod, ng, w_qkv, cos, sin, tm, rope):
    s, l, d = h.shape
    nq = N_HEADS * HEAD_DIM
    nk = N_KV_HEADS * HEAD_DIM
    tile = lambda w: pl.BlockSpec((1, tm, w), lambda i, t: (i, t, 0))
    return pl.pallas_call(
        functools.partial(_qkv_kernel, rope=rope),
        out_shape=(jax.ShapeDtypeStruct((s, l, nq), BF16),
                   jax.ShapeDtypeStruct((s, l, nk), BF16),
                   jax.ShapeDtypeStruct((s, l, nk), BF16)),
        grid=(s, l // tm),
        in_specs=[
            tile(d),
            pl.BlockSpec((1, 6, d), lambda i, t: (i, 0, 0)),
            _const_spec((1, d)),
            _const_spec(w_qkv.shape),
            pl.BlockSpec((tm, LANES), lambda i, t: (t, 0)),
            pl.BlockSpec((tm, LANES), lambda i, t: (t, 0)),
        ],
        out_specs=(tile(nq), tile(nk), tile(nk)),
        compiler_params=_cparams(2),
        name="qkv_rope" if rope else "qkv_ctx",
    )(h, mod, ng, w_qkv, cos, sin)


def _split_kv_head(x, g):
    col = x[:, (g // 2) * LANES:(g // 2 + 1) * LANES].astype(F32)
    lane = lax.broadcasted_iota(jnp.int32, col.shape, 1)
    rolled = pltpu.roll(col, HEAD_DIM, 1)
    low = lane < HEAD_DIM
    if g % 2 == 0:
        lo = jnp.where(low, col, 0.0)
        hi = jnp.where(low, 0.0, rolled)
    else:
        lo = jnp.where(low, rolled, 0.0)
        hi = jnp.where(low, 0.0, col)
    return lo.astype(BF16), hi.astype(BF16)


def _attn_kernel(*refs, has_band, seq_len):
    if has_band:
        (q_ref, kp_ref, kc_ref, kn_ref, vp_ref, vc_ref, vn_ref, kx_ref, vx_ref, sink_ref,
         h_ref, mod_ref, wo_ref, n2g_ref, hout_ref, b_ref) = refs
    else:
        (q_ref, kx_ref, vx_ref, sink_ref, h_ref, mod_ref, wo_ref, n2g_ref, hout_ref, b_ref) = refs
    tq = q_ref.shape[1]
    n_ctx = kx_ref.shape[1]

    if has_band:
        i = pl.program_id(1)
        k_all = jnp.concatenate([kp_ref[0], kc_ref[0], kn_ref[0], kx_ref[0]], axis=0)
        v_all = jnp.concatenate([vp_ref[0], vc_ref[0], vn_ref[0], vx_ref[0]], axis=0)
        n_band = 3 * tq
        row = lax.broadcasted_iota(jnp.int32, (2 * tq, n_band + n_ctx), 0)
        qpos = i * tq + jnp.where(row >= tq, row - tq, row)
        col = lax.broadcasted_iota(jnp.int32, (2 * tq, n_band + n_ctx), 1)
        kpos = (i - 1) * tq + col
        valid = (col >= n_band) | ((jnp.abs(qpos - kpos) <= WINDOW) & (kpos >= 0) & (kpos < seq_len))
    else:
        k_all = kx_ref[0]
        v_all = vx_ref[0]
        valid = None

    row_hi = lax.broadcasted_iota(jnp.int32, (2 * tq, 1), 0) >= tq
    dims = (((1,), (1,)), ((), ()))
    cols = []
    for g in range(N_KV_HEADS):
        qs = jnp.concatenate([q_ref[0, :, (2 * g) * LANES:(2 * g + 1) * LANES],
                              q_ref[0, :, (2 * g + 1) * LANES:(2 * g + 2) * LANES]], axis=0)
        k_halves = _split_kv_head(k_all, g)
        v_halves = _split_kv_head(v_all, g)
        o = None
        for half in range(2):
            s = lax.dot_general(qs, k_halves[half], dims, preferred_element_type=F32)
            if valid is not None:
                s = jnp.where(valid, s, NEG)
            snk = jnp.where(row_hi, sink_ref[0, 4 * g + 2 + half], sink_ref[0, 4 * g + half])
            m = jnp.maximum(jnp.max(s, axis=-1, keepdims=True), snk)
            p = jnp.exp(s - m)
            denom = jnp.sum(p, axis=-1, keepdims=True) + jnp.exp(snk - m)
            pv = jnp.dot(p.astype(BF16), v_halves[half], preferred_element_type=F32)
            pv = pv * (1.0 / denom)
            o = pv if o is None else o + pv
        cols.append(o)
    o_all = jnp.concatenate([cols[g][r * tq:(r + 1) * tq] for g in range(N_KV_HEADS) for r in range(2)],
                            axis=1)
    y = jnp.dot(o_all.astype(BF16), wo_ref[...], preferred_element_type=F32)
    hn = h_ref[0] + mod_ref[0, 2:3] * y
    hout_ref[0] = hn
    b_ref[0] = _rms_mod(hn, n2g_ref[...], mod_ref[0, 3:4], mod_ref[0, 4:5]).astype(BF16)


def _attention(q, k, v, kx, vx, sink, h, mod, w_o, n2g, has_band):
    s, l, d = h.shape
    nq = q.shape[2]
    nk = kx.shape[2]
    n_ctx = kx.shape[1]
    tq = BLOCK if has_band else l
    nb = l // tq
    tile = lambda w: pl.BlockSpec((1, tq, w), lambda i, t: (i, t, 0))
    prev = pl.BlockSpec((1, tq, nk), lambda i, t: (i, jnp.maximum(t - 1, 0), 0))
    nxt = pl.BlockSpec((1, tq, nk), lambda i, t: (i, jnp.minimum(t + 1, nb - 1), 0))
    ctx_spec = pl.BlockSpec((1, n_ctx, nk), lambda i, t: (i, 0, 0))
    band_specs = [prev, tile(nk), nxt, prev, tile(nk), nxt] if has_band else []
    band_args = [k, k, k, v, v, v] if has_band else []
    return pl.pallas_call(
        functools.partial(_attn_kernel, has_band=has_band, seq_len=l),
        out_shape=(jax.ShapeDtypeStruct((s, l, d), F32), jax.ShapeDtypeStruct((s, l, d), BF16)),
        grid=(s, nb),
        in_specs=[tile(nq)] + band_specs + [
            ctx_spec, ctx_spec,
            pl.BlockSpec(memory_space=pltpu.SMEM),
            tile(d),
            pl.BlockSpec((1, 6, d), lambda i, t: (i, 0, 0)),
            _const_spec(w_o.shape),
            _const_spec((1, d)),
        ],
        out_specs=(tile(d), tile(d)),
        compiler_params=_cparams(2),
        name="window_attn" if has_band else "ctx_attn",
    )(q, *band_args, kx, vx, sink, h, mod, w_o, n2g)


def _rope_tables(seq_len):
    rows = seq_len // GRID_W
    row = jnp.broadcast_to(jnp.arange(rows)[:, None], (rows, GRID_W)).reshape(-1).astype(F32)
    col = jnp.broadcast_to(jnp.arange(GRID_W)[None, :], (rows, GRID_W)).reshape(-1).astype(F32)
    n_freq = HEAD_DIM // 4
    inv = ROPE_BASE ** (-jnp.arange(n_freq, dtype=F32) / n_freq)
    ang = jnp.concatenate([row[:, None] * inv, col[:, None] * inv], axis=-1)
    cos, sin = jnp.cos(ang), jnp.sin(ang)
    reps = LANES // HEAD_DIM
    return (jnp.tile(jnp.concatenate([cos, cos], axis=-1), (1, reps)),
            jnp.tile(jnp.concatenate([-sin, sin], axis=-1), (1, reps)))


def _ffn_weights(w_up, w_dw, b_dw, w_down, cf):
    d, two_f = w_up.shape
    f = two_f // 2
    nc = f // cf
    wup = w_up.astype(BF16).reshape(d, 2, nc, cf).transpose(1, 2, 0, 3)
    wdw = w_dw.reshape(w_dw.shape[0], 2, nc, cf).transpose(1, 2, 0, 3)
    bdw = b_dw.reshape(2, nc, 1, cf)
    wdn = w_down.astype(BF16).reshape(nc, cf, w_down.shape[1])
    return wup, wdw, bdw, wdn


def kernel(x, c, ctx, c_ctx, ada_w, ada_b, norm1_g, norm2_g, final_g, conv_w_pw1, conv_b_pw1, conv_w_dw,
           conv_b_dw, conv_ln_g, conv_ln_b, conv_w_pw2, conv_b_pw2, attn_w_qkv, attn_w_o, attn_sink,
           ffn_w_up, ffn_w_dw, ffn_b_dw, ffn_w_down):
    bsz, seq, d = x.shape
    n_ctx = ctx.shape[1]
    depth = ada_w.shape[0]
    tm_lat = min(512, seq)
    tm_conv = min(256, seq)
    tm_ctx = min(256, n_ctx)
    cf = 256

    rows = -(-(bsz + 1) // 8) * 8
    c_all = jnp.concatenate([c, c_ctx[None, :], jnp.zeros((rows - bsz - 1, d), F32)], axis=0)
    mod_all = _modulation(c_all, ada_w, ada_b)
    cos, sin = _rope_tables(seq)
    final_g2 = final_g.reshape(1, d)

    h_lat, h_ctx = x, ctx
    for i in range(depth):
        last = i == depth - 1
        j = i // 2
        mod_lat = mod_all[i, :bsz].reshape(bsz, 6, d)
        mod_ctx = jnp.broadcast_to(mod_all[i, bsz].reshape(1, 6, d), (bsz, 6, d))
        n1g = norm1_g[i].reshape(1, d)
        n2g = norm2_g[i].reshape(1, d)
        ffn_w = _ffn_weights(ffn_w_up[i], ffn_w_dw[i], ffn_b_dw[i], ffn_w_down[i], cf)
        if i % 2 == 0:
            w1 = conv_w_pw1[j].astype(BF16)
            b1 = conv_b_pw1[j].reshape(1, 2 * d)
            w2 = conv_w_pw2[j].astype(BF16)
            conv_args = (conv_w_dw[j], conv_b_dw[j], conv_ln_g[j].reshape(1, d), conv_ln_b[j].reshape(1, d),
                         w2, conv_b_pw2[j].reshape(1, d), n2g)
            v_lat = _conv_front(h_lat, mod_lat, n1g, w1, b1, tm_lat)
            h_lat, b_lat = _conv_back(v_lat, h_lat, mod_lat, *conv_args, tm_conv)
            if not last:
                v_ctx = _conv_front(h_ctx, mod_ctx, n1g, w1, b1, tm_ctx)
                h_ctx, b_ctx = _conv_back(v_ctx, h_ctx, mod_ctx, *conv_args, tm_ctx)
        else:
            wqkv = attn_w_qkv[j].astype(BF16)
            wo = attn_w_o[j].astype(BF16)
            sink = attn_sink[j].reshape(1, N_HEADS)
            qc, kc, vc = _qkv(h_ctx, mod_ctx, n1g, wqkv, cos[:n_ctx], sin[:n_ctx], tm_ctx, rope=False)
            q, k, v = _qkv(h_lat, mod_lat, n1g, wqkv, cos, sin, tm_lat, rope=True)
            h_lat, b_lat = _attention(q, k, v, kc, vc, sink, h_lat, mod_lat, wo, n2g, has_band=True)
            if not last:
                h_ctx, b_ctx = _attention(qc, None, None, kc, vc, sink, h_ctx, mod_ctx, wo, n2g, has_band=False)
        h_lat = _ffn(b_lat, h_lat, mod_lat, *ffn_w, final_g2, tm_lat, final_norm=last)
        if not last:
            h_ctx = _ffn(b_ctx, h_ctx, mod_ctx, *ffn_w, final_g2, tm_ctx, final_norm=False)
    return h_lat
```

```python
import functools

import jax
import jax.numpy as jnp
from jax import lax
from jax.experimental import pallas as pl
from jax.experimental.pallas import tpu as pltpu

F32 = jnp.float32
BF16 = jnp.bfloat16

EPS = 1e-6
NEG = -1e30
HEAD_DIM = 64
N_HEADS = 16
N_KV_HEADS = 4
WINDOW = 128
BLOCK = 128
GRID_W = 64
ROPE_BASE = 10000.0

LANES = 128
HALO = 16
VMEM_LIMIT = 56 * 1024 * 1024


def _cparams(n_axes):
    return pltpu.CompilerParams(
        dimension_semantics=("arbitrary",) * n_axes,
        vmem_limit_bytes=VMEM_LIMIT,
    )


def _const_spec(shape):
    nd = len(shape)
    return pl.BlockSpec(shape, lambda *_: (0,) * nd, pipeline_mode=pl.Buffered(1))


def _rms_mod(x, g, shift, scale):
    ms = jnp.mean(x * x, axis=-1, keepdims=True)
    y = x * lax.rsqrt(ms + EPS) * g
    return y * (1.0 + scale) + shift


def _rms(x, g):
    ms = jnp.mean(x * x, axis=-1, keepdims=True)
    return x * lax.rsqrt(ms + EPS) * g


def _mod_kernel(c_ref, w_ref, b_ref, o_ref):
    c = c_ref[...]
    sc = c * jax.nn.sigmoid(c)
    o_ref[0] = jnp.dot(sc, w_ref[0], preferred_element_type=F32,
                       precision=lax.Precision.HIGHEST) + b_ref[0]


def _modulation(c_all, ada_w, ada_b):
    depth, d, n = ada_w.shape
    rows = c_all.shape[0]
    nb = 1536
    return pl.pallas_call(
        _mod_kernel,
        out_shape=jax.ShapeDtypeStruct((depth, rows, n), F32),
        grid=(depth, n // nb),
        in_specs=[
            pl.BlockSpec((rows, d), lambda i, j: (0, 0)),
            pl.BlockSpec((1, d, nb), lambda i, j: (i, 0, j)),
            pl.BlockSpec((1, 1, nb), lambda i, j: (i, 0, j)),
        ],
        out_specs=pl.BlockSpec((1, rows, nb), lambda i, j: (i, 0, j)),
        compiler_params=_cparams(2),
        name="adaln_mod",
    )(c_all, ada_w, ada_b.reshape(depth, 1, n))


def _conv_front_kernel(h_ref, mod_ref, ng_ref, w_ref, b_ref, v_ref):
    x = h_ref[0]
    d = x.shape[-1]
    a = _rms_mod(x, ng_ref[...], mod_ref[0, 0:1], mod_ref[0, 1:2])
    u = jnp.dot(a.astype(BF16), w_ref[...], preferred_element_type=F32) + b_ref[...]
    v_ref[0] = u[:, :d] * jax.nn.sigmoid(u[:, d:])


def _conv_front(h, mod, ng, w_pw1, b_pw1, tm):
    s, l, d = h.shape
    return pl.pallas_call(
        _conv_front_kernel,
        out_shape=jax.ShapeDtypeStruct((s, l, d), F32),
        grid=(s, l // tm),
        in_specs=[
            pl.BlockSpec((1, tm, d), lambda i, t: (i, t, 0)),
            pl.BlockSpec((1, 6, d), lambda i, t: (i, 0, 0)),
            _const_spec((1, d)),
            _const_spec((d, 2 * d)),
            _const_spec((1, 2 * d)),
        ],
        out_specs=pl.BlockSpec((1, tm, d), lambda i, t: (i, t, 0)),
        compiler_params=_cparams(2),
        name="conv_front",
    )(h, mod, ng, w_pw1, b_pw1)


def _halo_specs(tm, width, n_tiles):
    r = tm // HALO

    def prev_map(i, t):
        return (i, jnp.maximum(t * r - 1, 0), 0)

    def next_map(i, t):
        return (i, jnp.minimum((t + 1) * r, n_tiles * r - 1), 0)

    return [
        pl.BlockSpec((1, HALO, width), prev_map),
        pl.BlockSpec((1, tm, width), lambda i, t: (i, t, 0)),
        pl.BlockSpec((1, HALO, width), next_map),
    ]


def _conv_back_kernel(vp_ref, v_ref, vn_ref, h_ref, mod_ref, wdw_ref, bdw_ref, lng_ref, lnb_ref,
                      w2_ref, b2_ref, n2g_ref, hout_ref, b_ref, vext_ref, conv_ref, *, n_taps):
    t = pl.program_id(1)
    nt = pl.num_programs(1)
    tm = v_ref.shape[1]
    d = v_ref.shape[2]
    ncb = d // LANES
    rc = 64

    vp = jnp.where(t > 0, vp_ref[0], 0.0)
    vn = jnp.where(t < nt - 1, vn_ref[0], 0.0)
    for cb in range(ncb):
        cs = slice(cb * LANES, (cb + 1) * LANES)
        vext_ref[cb, 0:HALO, :] = vp[:, cs]
        vext_ref[cb, HALO:HALO + tm, :] = v_ref[0, :, cs]
        vext_ref[cb, HALO + tm:, :] = vn[:, cs]

    base = HALO - (n_taps - 1) // 2

    def col_block(cb, carry):
        bias = bdw_ref[cb]
        for rb in range(tm // rc):
            acc = jnp.zeros((rc, LANES), F32) + bias
            for k in range(n_taps):
                off = rb * rc + base + k
                acc = acc + vext_ref[cb, off:off + rc, :] * wdw_ref[cb, k:k + 1, :]
            conv_ref[cb, rb * rc:(rb + 1) * rc, :] = acc
        return carry

    lax.fori_loop(0, ncb, col_block, 0)

    y = jnp.concatenate([conv_ref[cb] for cb in range(ncb)], axis=1)
    mu = jnp.mean(y, axis=-1, keepdims=True)
    yc = y - mu
    var = jnp.mean(yc * yc, axis=-1, keepdims=True)
    z = yc * lax.rsqrt(var + EPS) * lng_ref[...] + lnb_ref[...]
    z = z * jax.nn.sigmoid(z)
    o = jnp.dot(z.astype(BF16), w2_ref[...], preferred_element_type=F32) + b2_ref[...]
    hn = h_ref[0] + mod_ref[0, 2:3] * o
    hout_ref[0] = hn
    b_ref[0] = _rms_mod(hn, n2g_ref[...], mod_ref[0, 3:4], mod_ref[0, 4:5]).astype(BF16)


def _conv_back(v, h, mod, w_dw, b_dw, ln_g, ln_b, w_pw2, b_pw2, n2g, tm):
    s, l, d = h.shape
    n_taps = w_dw.shape[0]
    ncb = d // LANES
    taps_pad = -(-n_taps // 8) * 8
    wdw = jnp.pad(w_dw, ((0, taps_pad - n_taps), (0, 0))).reshape(taps_pad, ncb, LANES).transpose(1, 0, 2)
    bdw = b_dw.reshape(ncb, 1, LANES)
    tile = pl.BlockSpec((1, tm, d), lambda i, t: (i, t, 0))
    return pl.pallas_call(
        functools.partial(_conv_back_kernel, n_taps=n_taps),
        out_shape=(jax.ShapeDtypeStruct((s, l, d), F32), jax.ShapeDtypeStruct((s, l, d), BF16)),
        grid=(s, l // tm),
        in_specs=_halo_specs(tm, d, l // tm) + [
            tile,
            pl.BlockSpec((1, 6, d), lambda i, t: (i, 0, 0)),
            _const_spec((ncb, taps_pad, LANES)),
            _const_spec((ncb, 1, LANES)),
            _const_spec((1, d)),
            _const_spec((1, d)),
            _const_spec((d, d)),
            _const_spec((1, d)),
            _const_spec((1, d)),
        ],
        out_specs=(tile, tile),
        scratch_shapes=[
            pltpu.VMEM((ncb, tm + 2 * HALO, LANES), F32),
            pltpu.VMEM((ncb, tm, LANES), F32),
        ],
        compiler_params=_cparams(2),
        name="conv_back",
    )(v, v, v, h, mod, wdw, bdw, ln_g, ln_b, w_pw2, b_pw2, n2g)


def _ffn_kernel(xp_ref, x_ref, xn_ref, h_ref, mod_ref, wup_ref, wdw_ref, bdw_ref, wdn_ref, fg_ref,
                out_ref, xs_ref, y_ref, *, final_norm):
    t = pl.program_id(1)
    nt = pl.num_programs(1)
    tm = x_ref.shape[1]
    n_chunks, _, cf = wdw_ref.shape[1:]
    rows = tm + 2 * HALO

    xs_ref[0:HALO, :] = jnp.where(t > 0, xp_ref[0], jnp.zeros_like(xp_ref[0]))
    xs_ref[HALO:HALO + tm, :] = x_ref[0]
    xs_ref[HALO + tm:, :] = jnp.where(t < nt - 1, xn_ref[0], jnp.zeros_like(xn_ref[0]))

    def dw(u, half, c):
        w = wdw_ref[half, c]
        return (pltpu.roll(u, 1, 0)[HALO:HALO + tm] * w[0:1]
                + u[HALO:HALO + tm] * w[1:2]
                + pltpu.roll(u, rows - 1, 0)[HALO:HALO + tm] * w[2:3]
                + bdw_ref[half, c])

    for c in range(n_chunks):
        xs = xs_ref[...]
        ca = dw(jnp.dot(xs, wup_ref[0, c], preferred_element_type=F32), 0, c)
        cb = dw(jnp.dot(xs, wup_ref[1, c], preferred_element_type=F32), 1, c)
        y_ref[:, c * cf:(c + 1) * cf] = ((ca * jax.nn.sigmoid(ca)) * cb).astype(BF16)

    acc = jnp.dot(y_ref[...], wdn_ref[...], preferred_element_type=F32)
    hn = h_ref[0] + mod_ref[0, 5:6] * acc
    if final_norm:
        hn = _rms(hn, fg_ref[...])
    out_ref[0] = hn


def _ffn(b, h, mod, wup, wdw, bdw, wdn, final_g, tm, final_norm):
    s, l, d = h.shape
    f = wdn.shape[0]
    tile = pl.BlockSpec((1, tm, d), lambda i, t: (i, t, 0))
    return pl.pallas_call(
        functools.partial(_ffn_kernel, final_norm=final_norm),
        out_shape=jax.ShapeDtypeStruct((s, l, d), F32),
        grid=(s, l // tm),
        in_specs=_halo_specs(tm, d, l // tm) + [
            tile,
            pl.BlockSpec((1, 6, d), lambda i, t: (i, 0, 0)),
            _const_spec(wup.shape),
            _const_spec(wdw.shape),
            _const_spec(bdw.shape),
            _const_spec(wdn.shape),
            _const_spec((1, d)),
        ],
        out_specs=tile,
        scratch_shapes=[
            pltpu.VMEM((tm + 2 * HALO, d), BF16),
            pltpu.VMEM((tm, f), BF16),
        ],
        compiler_params=_cparams(2),
        name="conv_ffn",
    )(b, b, b, h, mod, wup, wdw, bdw, wdn, final_g)


def _swap_halves(x):
    lane = lax.broadcasted_iota(jnp.int32, x.shape, 1)
    return jnp.where((lane & (HEAD_DIM - 1)) < HEAD_DIM // 2,
                     pltpu.roll(x, LANES - HEAD_DIM // 2, 1),
                     pltpu.roll(x, HEAD_DIM // 2, 1))


def _qkv_kernel(h_ref, mod_ref, ng_ref, w_ref, cos_ref, sin_ref, q_ref, k_ref, v_ref, *, rope):
    x = h_ref[0]
    a = _rms_mod(x, ng_ref[...], mod_ref[0, 0:1], mod_ref[0, 1:2])
    qkv = jnp.dot(a.astype(BF16), w_ref[...], preferred_element_type=F32)
    nq = q_ref.shape[2]
    nk = k_ref.shape[2]
    scale = HEAD_DIM ** -0.5
    if rope:
        cos = cos_ref[...]
        sin = sin_ref[...]
        blocks = []
        for j in range((nq + nk) // LANES):
            blk = qkv[:, j * LANES:(j + 1) * LANES]
            blocks.append(blk * cos + _swap_halves(blk) * sin)
        for j in range(nq // LANES):
            q_ref[0, :, j * LANES:(j + 1) * LANES] = (blocks[j] * scale).astype(BF16)
        for j in range(nk // LANES):
            k_ref[0, :, j * LANES:(j + 1) * LANES] = blocks[nq // LANES + j].astype(BF16)
    else:
        q_ref[0] = (qkv[:, :nq] * scale).astype(BF16)
        k_ref[0] = qkv[:, nq:nq + nk].astype(BF16)
    v_ref[0] = qkv[:, nq + nk:].astype(BF16)


def _qkv(h, mod, ng, w_qkv, cos, sin, tm, rope):
    s, l, d = h.shape
    nq = N_HEADS * HEAD_DIM
    nk = N_KV_HEADS * HEAD_DIM
    tile = lambda w: pl.BlockSpec((1, tm, w), lambda i, t: (i, t, 0))
    return pl.pallas_call(
        functools.partial(_qkv_kernel, rope=rope),
        out_shape=(jax.ShapeDtypeStruct((s, l, nq), BF16),
                   jax.ShapeDtypeStruct((s, l, nk), BF16),
                   jax.ShapeDtypeStruct((s, l, nk), BF16)),
        grid=(s, l // tm),
        in_specs=[
            tile(d),
            pl.BlockSpec((1, 6, d), lambda i, t: (i, 0, 0)),
            _const_spec((1, d)),
            _const_spec(w_qkv.shape),
            pl.BlockSpec((tm, LANES), lambda i, t: (t, 0)),
            pl.BlockSpec((tm, LANES), lambda i, t: (t, 0)),
        ],
        out_specs=(tile(nq), tile(nk), tile(nk)),
        compiler_params=_cparams(2),
        name="qkv_rope" if rope else "qkv_ctx",
    )(h, mod, ng, w_qkv, cos, sin)


def _paired_head_order():
    group = N_HEADS // N_KV_HEADS
    order = []
    for c in range(N_KV_HEADS // 2):
        for r in range(group):
            order += [(2 * c) * group + r, (2 * c + 1) * group + r]
    return order


def _attn_kernel(*refs, has_band):
    if has_band:
        (q_ref, kp_ref, kc_ref, kn_ref, vp_ref, vc_ref, vn_ref, kx_ref, vx_ref, sink_ref,
         h_ref, mod_ref, wo_ref, n2g_ref, hout_ref, b_ref, s_ref, p_ref) = refs
    else:
        (q_ref, kx_ref, vx_ref, sink_ref, h_ref, mod_ref, wo_ref, n2g_ref, hout_ref, b_ref,
         s_ref, p_ref) = refs
    tq = q_ref.shape[1]
    n_ctx = kx_ref.shape[1]
    group = N_HEADS // N_KV_HEADS

    if has_band:
        i = pl.program_id(1)
        k_all = jnp.concatenate([kp_ref[0], kc_ref[0], kn_ref[0], kx_ref[0]], axis=0)
        v_all = jnp.concatenate([vp_ref[0], vc_ref[0], vn_ref[0], vx_ref[0]], axis=0)
        row = lax.broadcasted_iota(jnp.int32, (2 * tq, tq), 0)
        row = jnp.where(row >= tq, row - tq, row)
        col = lax.broadcasted_iota(jnp.int32, (2 * tq, tq), 1)
        valid_prev = (col >= row) & (i > 0)
        valid_next = (col <= row) & (i < pl.num_programs(1) - 1)
        assert tq == LANES
        prev_blk, next_blk = 0, 2
    else:
        k_all = kx_ref[0]
        v_all = vx_ref[0]
        prev_blk = next_blk = None
    n_keys = s_ref.shape[1]

    low = lax.broadcasted_iota(jnp.int32, (tq, LANES), 1) < HEAD_DIM
    row_hi = lax.broadcasted_iota(jnp.int32, (2 * tq, 1), 0) >= tq
    dims = (((1,), (1,)), ((), ()))
    zero = jnp.zeros((tq, LANES), BF16)
    cols = []
    for c in range(N_KV_HEADS // 2):
        kcol = k_all[:, c * LANES:(c + 1) * LANES]
        vcol = v_all[:, c * LANES:(c + 1) * LANES]
        parts = []
        for r in range(group):
            qcol = q_ref[0, :, (group * c + r) * LANES:(group * c + r + 1) * LANES]
            parts += [jnp.where(low, qcol, zero), jnp.where(low, zero, qcol)]
        s_ref[...] = lax.dot_general(jnp.concatenate(parts, axis=0), kcol, dims, preferred_element_type=F32)
        inv = []
        for r in range(group):
            rows = slice(r * 2 * tq, (r + 1) * 2 * tq)
            head = 2 * (group * c + r)
            snk = jnp.where(row_hi, sink_ref[0, head + 1], sink_ref[0, head])
            blocks = []
            for n in range(n_keys // LANES):
                blk = s_ref[rows, n * LANES:(n + 1) * LANES]
                if n == prev_blk:
                    blk = jnp.where(valid_prev, blk, NEG)
                if n == next_blk:
                    blk = jnp.where(valid_next, blk, NEG)
                blocks.append(blk)
            m = blocks[0]
            for blk in blocks[1:]:
                m = jnp.maximum(m, blk)
            m = jnp.maximum(jnp.max(m, axis=-1, keepdims=True), snk)
            acc = None
            for n, blk in enumerate(blocks):
                e = jnp.exp(blk - m)
                acc = e if acc is None else acc + e
                p_ref[rows, n * LANES:(n + 1) * LANES] = e.astype(BF16)
            denom = jnp.sum(acc, axis=-1, keepdims=True) + jnp.exp(snk - m)
            inv.append(1.0 / denom)
        o = jnp.dot(p_ref[...], vcol, preferred_element_type=F32)
        for r in range(group):
            blk = o[r * 2 * tq:(r + 1) * 2 * tq] * inv[r]
            cols.append(jnp.where(low, blk[0:tq], blk[tq:2 * tq]))
    o_all = jnp.concatenate(cols, axis=1)
    y = jnp.dot(o_all.astype(BF16), wo_ref[...], preferred_element_type=F32)
    hn = h_ref[0] + mod_ref[0, 2:3] * y
    hout_ref[0] = hn
    b_ref[0] = _rms_mod(hn, n2g_ref[...], mod_ref[0, 3:4], mod_ref[0, 4:5]).astype(BF16)


def _attention(q, k, v, kx, vx, sink, h, mod, w_o, n2g, has_band):
    s, l, d = h.shape
    nq = q.shape[2]
    nk = kx.shape[2]
    n_ctx = kx.shape[1]
    tq = BLOCK if has_band else l
    nb = l // tq
    n_keys = (3 * tq if has_band else 0) + n_ctx
    tile = lambda w: pl.BlockSpec((1, tq, w), lambda i, t: (i, t, 0))
    prev = pl.BlockSpec((1, tq, nk), lambda i, t: (i, jnp.maximum(t - 1, 0), 0))
    nxt = pl.BlockSpec((1, tq, nk), lambda i, t: (i, jnp.minimum(t + 1, nb - 1), 0))
    ctx_spec = pl.BlockSpec((1, n_ctx, nk), lambda i, t: (i, 0, 0))
    band_specs = [prev, tile(nk), nxt, prev, tile(nk), nxt] if has_band else []
    band_args = [k, k, k, v, v, v] if has_band else []
    return pl.pallas_call(
        functools.partial(_attn_kernel, has_band=has_band),
        out_shape=(jax.ShapeDtypeStruct((s, l, d), F32), jax.ShapeDtypeStruct((s, l, d), BF16)),
        grid=(s, nb),
        in_specs=[tile(nq)] + band_specs + [
            ctx_spec, ctx_spec,
            pl.BlockSpec(memory_space=pltpu.SMEM),
            tile(d),
            pl.BlockSpec((1, 6, d), lambda i, t: (i, 0, 0)),
            _const_spec(w_o.shape),
            _const_spec((1, d)),
        ],
        out_specs=(tile(d), tile(d)),
        compiler_params=_cparams(2),
        scratch_shapes=[
            pltpu.VMEM((2 * (N_HEADS // N_KV_HEADS) * tq, n_keys), F32),
            pltpu.VMEM((2 * (N_HEADS // N_KV_HEADS) * tq, n_keys), BF16),
        ],
        name="window_attn" if has_band else "ctx_attn",
    )(q, *band_args, kx, vx, sink, h, mod, w_o, n2g)


def _rope_tables(seq_len):
    rows = seq_len // GRID_W
    row = jnp.broadcast_to(jnp.arange(rows)[:, None], (rows, GRID_W)).reshape(-1).astype(F32)
    col = jnp.broadcast_to(jnp.arange(GRID_W)[None, :], (rows, GRID_W)).reshape(-1).astype(F32)
    n_freq = HEAD_DIM // 4
    inv = ROPE_BASE ** (-jnp.arange(n_freq, dtype=F32) / n_freq)
    ang = jnp.concatenate([row[:, None] * inv, col[:, None] * inv], axis=-1)
    cos, sin = jnp.cos(ang), jnp.sin(ang)
    reps = LANES // HEAD_DIM
    return (jnp.tile(jnp.concatenate([cos, cos], axis=-1), (1, reps)),
            jnp.tile(jnp.concatenate([-sin, sin], axis=-1), (1, reps)))


def _ffn_weights(w_up, w_dw, b_dw, w_down, cf):
    d, two_f = w_up.shape
    f = two_f // 2
    nc = f // cf
    wup = w_up.astype(BF16).reshape(d, 2, nc, cf).transpose(1, 2, 0, 3)
    wdw = w_dw.reshape(w_dw.shape[0], 2, nc, cf).transpose(1, 2, 0, 3)
    bdw = b_dw.reshape(2, nc, 1, cf)
    return wup, wdw, bdw, w_down.astype(BF16)


def kernel(x, c, ctx, c_ctx, ada_w, ada_b, norm1_g, norm2_g, final_g, conv_w_pw1, conv_b_pw1, conv_w_dw,
           conv_b_dw, conv_ln_g, conv_ln_b, conv_w_pw2, conv_b_pw2, attn_w_qkv, attn_w_o, attn_sink,
           ffn_w_up, ffn_w_dw, ffn_b_dw, ffn_w_down):
    bsz, seq, d = x.shape
    n_ctx = ctx.shape[1]
    depth = ada_w.shape[0]
    tm_lat = min(512, seq)
    tm_conv = min(256, seq)
    tm_ctx = min(256, n_ctx)
    cf = 256

    rows = -(-(bsz + 1) // 8) * 8
    c_all = jnp.concatenate([c, c_ctx[None, :], jnp.zeros((rows - bsz - 1, d), F32)], axis=0)
    mod_all = _modulation(c_all, ada_w, ada_b)
    cos, sin = _rope_tables(seq)
    final_g2 = final_g.reshape(1, d)

    h_lat, h_ctx = x, ctx
    for i in range(depth):
        last = i == depth - 1
        j = i // 2
        mod_lat = mod_all[i, :bsz].reshape(bsz, 6, d)
        mod_ctx = jnp.broadcast_to(mod_all[i, bsz].reshape(1, 6, d), (bsz, 6, d))
        n1g = norm1_g[i].reshape(1, d)
        n2g = norm2_g[i].reshape(1, d)
        ffn_w = _ffn_weights(ffn_w_up[i], ffn_w_dw[i], ffn_b_dw[i], ffn_w_down[i], cf)
        if i % 2 == 0:
            w1 = conv_w_pw1[j].astype(BF16)
            b1 = conv_b_pw1[j].reshape(1, 2 * d)
            w2 = conv_w_pw2[j].astype(BF16)
            conv_args = (conv_w_dw[j], conv_b_dw[j], conv_ln_g[j].reshape(1, d), conv_ln_b[j].reshape(1, d),
                         w2, conv_b_pw2[j].reshape(1, d), n2g)
            v_lat = _conv_front(h_lat, mod_lat, n1g, w1, b1, tm_lat)
            h_lat, b_lat = _conv_back(v_lat, h_lat, mod_lat, *conv_args, tm_conv)
            if not last:
                v_ctx = _conv_front(h_ctx, mod_ctx, n1g, w1, b1, tm_ctx)
                h_ctx, b_ctx = _conv_back(v_ctx, h_ctx, mod_ctx, *conv_args, tm_ctx)
        else:
            order = jnp.asarray(_paired_head_order())
            q_cols = (order[:, None] * HEAD_DIM + jnp.arange(HEAD_DIM)[None, :]).reshape(-1)
            nq = N_HEADS * HEAD_DIM
            wqkv = jnp.concatenate([attn_w_qkv[j][:, :nq][:, q_cols], attn_w_qkv[j][:, nq:]], axis=1).astype(BF16)
            wo = attn_w_o[j][q_cols, :].astype(BF16)
            sink = attn_sink[j][order].reshape(1, N_HEADS)
            qc, kc, vc = _qkv(h_ctx, mod_ctx, n1g, wqkv, cos[:n_ctx], sin[:n_ctx], tm_ctx, rope=False)
            q, k, v = _qkv(h_lat, mod_lat, n1g, wqkv, cos, sin, tm_lat, rope=True)
            h_lat, b_lat = _attention(q, k, v, kc, vc, sink, h_lat, mod_lat, wo, n2g, has_band=True)
            if not last:
                h_ctx, b_ctx = _attention(qc, None, None, kc, vc, sink, h_ctx, mod_ctx, wo, n2g, has_band=False)
        h_lat = _ffn(b_lat, h_lat, mod_lat, *ffn_w, final_g2, tm_lat, final_norm=last)
        if not last:
            h_ctx = _ffn(b_ctx, h_ctx, mod_ctx, *ffn_w, final_g2, tm_ctx, final_norm=False)
    return h_lat
```

```python
import functools

import jax
import jax.numpy as jnp
from jax import lax
from jax.experimental import pallas as pl
from jax.experimental.pallas import tpu as pltpu

F32 = jnp.float32
BF16 = jnp.bfloat16

EPS = 1e-6
NEG = -1e30
HEAD_DIM = 64
N_HEADS = 16
N_KV_HEADS = 4
WINDOW = 128
BLOCK = 128
GRID_W = 64
ROPE_BASE = 10000.0
LOG2E = 1.4426950408889634

LANES = 128
HALO = 16
VMEM_LIMIT = 56 * 1024 * 1024
MXU_DIM = 256
TILE_ROWS = 1024
CONV_TILE_ROWS = 512
SUB_ROWS = MXU_DIM
FFN_CHUNK = MXU_DIM
ATTN_SUB_BLOCKS = 2


def _cparams(n_axes):
    return pltpu.CompilerParams(
        dimension_semantics=("arbitrary",) * n_axes,
        vmem_limit_bytes=VMEM_LIMIT,
    )


def _const_spec(shape):
    nd = len(shape)
    return pl.BlockSpec(shape, lambda *_: (0,) * nd, pipeline_mode=pl.Buffered(1))


def _rms_mod(x, g, shift, scale):
    ms = jnp.mean(x * x, axis=-1, keepdims=True)
    y = x * lax.rsqrt(ms + EPS) * g
    return y * (1.0 + scale) + shift


def _rms(x, g):
    ms = jnp.mean(x * x, axis=-1, keepdims=True)
    return x * lax.rsqrt(ms + EPS) * g


def _mod_kernel(c_ref, w_ref, b_ref, o_ref):
    c = c_ref[...]
    sc = c * jax.nn.sigmoid(c)
    o_ref[0] = jnp.dot(sc, w_ref[0], preferred_element_type=F32,
                       precision=lax.Precision.HIGHEST) + b_ref[0]


def _modulation(c_all, ada_w, ada_b):
    depth, d, n = ada_w.shape
    rows = c_all.shape[0]
    nb = 1536
    return pl.pallas_call(
        _mod_kernel,
        out_shape=jax.ShapeDtypeStruct((depth, rows, n), F32),
        grid=(depth, n // nb),
        in_specs=[
            pl.BlockSpec((rows, d), lambda i, j: (0, 0)),
            pl.BlockSpec((1, d, nb), lambda i, j: (i, 0, j)),
            pl.BlockSpec((1, 1, nb), lambda i, j: (i, 0, j)),
        ],
        out_specs=pl.BlockSpec((1, rows, nb), lambda i, j: (i, 0, j)),
        compiler_params=_cparams(2),
        name="adaln_mod",
    )(c_all, ada_w, ada_b.reshape(depth, 1, n))


def _conv_front_kernel(h_ref, mod_ref, ng_ref, w_ref, b_ref, v_ref):
    tm, d = h_ref.shape[1:]
    for r0 in range(0, tm, SUB_ROWS):
        rows = slice(r0, min(r0 + SUB_ROWS, tm))
        a = _rms_mod(h_ref[0, rows], ng_ref[...], mod_ref[0, 0:1], mod_ref[0, 1:2])
        u = jnp.dot(a.astype(BF16), w_ref[...], preferred_element_type=F32) + b_ref[...]
        v_ref[0, rows] = u[:, :d] * jax.nn.sigmoid(u[:, d:])


def _conv_front(h, mod, ng, w_pw1, b_pw1, tm):
    s, l, d = h.shape
    return pl.pallas_call(
        _conv_front_kernel,
        out_shape=jax.ShapeDtypeStruct((s, l, d), F32),
        grid=(s, l // tm),
        in_specs=[
            pl.BlockSpec((1, tm, d), lambda i, t: (i, t, 0)),
            pl.BlockSpec((1, 6, d), lambda i, t: (i, 0, 0)),
            _const_spec((1, d)),
            _const_spec((d, 2 * d)),
            _const_spec((1, 2 * d)),
        ],
        out_specs=pl.BlockSpec((1, tm, d), lambda i, t: (i, t, 0)),
        compiler_params=_cparams(2),
        name="conv_front",
    )(h, mod, ng, w_pw1, b_pw1)


def _halo_specs(tm, width, n_tiles):
    r = tm // HALO

    def prev_map(i, t):
        return (i, jnp.maximum(t * r - 1, 0), 0)

    def next_map(i, t):
        return (i, jnp.minimum((t + 1) * r, n_tiles * r - 1), 0)

    return [
        pl.BlockSpec((1, HALO, width), prev_map),
        pl.BlockSpec((1, tm, width), lambda i, t: (i, t, 0)),
        pl.BlockSpec((1, HALO, width), next_map),
    ]


def _conv_back_kernel(vp_ref, v_ref, vn_ref, h_ref, mod_ref, wdw_ref, bdw_ref, lng_ref, lnb_ref,
                      w2_ref, b2_ref, n2g_ref, hout_ref, b_ref, vext_ref, conv_ref, *, n_taps):
    t = pl.program_id(1)
    nt = pl.num_programs(1)
    tm = v_ref.shape[1]
    d = v_ref.shape[2]
    ncb = d // LANES
    rc = 64

    vp = jnp.where(t > 0, vp_ref[0], 0.0)
    vn = jnp.where(t < nt - 1, vn_ref[0], 0.0)
    for cb in range(ncb):
        cs = slice(cb * LANES, (cb + 1) * LANES)
        vext_ref[cb, 0:HALO, :] = vp[:, cs]
        vext_ref[cb, HALO:HALO + tm, :] = v_ref[0, :, cs]
        vext_ref[cb, HALO + tm:, :] = vn[:, cs]

    base = HALO - (n_taps - 1) // 2

    def col_block(cb, carry):
        bias = bdw_ref[cb]
        for rb in range(tm // rc):
            acc = jnp.zeros((rc, LANES), F32) + bias
            for k in range(n_taps):
                off = rb * rc + base + k
                acc = acc + vext_ref[cb, off:off + rc, :] * wdw_ref[cb, k:k + 1, :]
            conv_ref[cb, rb * rc:(rb + 1) * rc, :] = acc
        return carry

    lax.fori_loop(0, ncb, col_block, 0)

    y = jnp.concatenate([conv_ref[cb] for cb in range(ncb)], axis=1)
    mu = jnp.mean(y, axis=-1, keepdims=True)
    yc = y - mu
    var = jnp.mean(yc * yc, axis=-1, keepdims=True)
    z = yc * lax.rsqrt(var + EPS) * lng_ref[...] + lnb_ref[...]
    z = z * jax.nn.sigmoid(z)
    o = jnp.dot(z.astype(BF16), w2_ref[...], preferred_element_type=F32) + b2_ref[...]
    hn = h_ref[0] + mod_ref[0, 2:3] * o
    hout_ref[0] = hn
    b_ref[0] = _rms_mod(hn, n2g_ref[...], mod_ref[0, 3:4], mod_ref[0, 4:5]).astype(BF16)


def _conv_back(v, h, mod, w_dw, b_dw, ln_g, ln_b, w_pw2, b_pw2, n2g, tm):
    s, l, d = h.shape
    n_taps = w_dw.shape[0]
    ncb = d // LANES
    taps_pad = -(-n_taps // 8) * 8
    wdw = jnp.pad(w_dw, ((0, taps_pad - n_taps), (0, 0))).reshape(taps_pad, ncb, LANES).transpose(1, 0, 2)
    bdw = b_dw.reshape(ncb, 1, LANES)
    tile = pl.BlockSpec((1, tm, d), lambda i, t: (i, t, 0))
    return pl.pallas_call(
        functools.partial(_conv_back_kernel, n_taps=n_taps),
        out_shape=(jax.ShapeDtypeStruct((s, l, d), F32), jax.ShapeDtypeStruct((s, l, d), BF16)),
        grid=(s, l // tm),
        in_specs=_halo_specs(tm, d, l // tm) + [
            tile,
            pl.BlockSpec((1, 6, d), lambda i, t: (i, 0, 0)),
            _const_spec((ncb, taps_pad, LANES)),
            _const_spec((ncb, 1, LANES)),
            _const_spec((1, d)),
            _const_spec((1, d)),
            _const_spec((d, d)),
            _const_spec((1, d)),
            _const_spec((1, d)),
        ],
        out_specs=(tile, tile),
        scratch_shapes=[
            pltpu.VMEM((ncb, tm + 2 * HALO, LANES), F32),
            pltpu.VMEM((ncb, tm, LANES), F32),
        ],
        compiler_params=_cparams(2),
        name="conv_back",
    )(v, v, v, h, mod, wdw, bdw, ln_g, ln_b, w_pw2, b_pw2, n2g)


def _ffn_kernel(xp_ref, x_ref, xn_ref, h_ref, mod_ref, wup_ref, wdw_ref, bdw_ref, wdn_ref, fg_ref,
                out_ref, xs_ref, y_ref, *, final_norm):
    t = pl.program_id(1)
    nt = pl.num_programs(1)
    tm = x_ref.shape[1]
    n_chunks, _, cf = wdw_ref.shape[1:]
    rows = tm + 2 * HALO

    xs_ref[0:HALO, :] = jnp.where(t > 0, xp_ref[0], jnp.zeros_like(xp_ref[0]))
    xs_ref[HALO:HALO + tm, :] = x_ref[0]
    xs_ref[HALO + tm:, :] = jnp.where(t < nt - 1, xn_ref[0], jnp.zeros_like(xn_ref[0]))

    def dw(u, half, c):
        w = wdw_ref[half, c]
        return (pltpu.roll(u, 1, 0)[HALO:HALO + tm] * w[0:1]
                + u[HALO:HALO + tm] * w[1:2]
                + pltpu.roll(u, rows - 1, 0)[HALO:HALO + tm] * w[2:3]
                + bdw_ref[half, c])

    for c in range(n_chunks):
        xs = xs_ref[...]
        ca = dw(jnp.dot(xs, wup_ref[0, c], preferred_element_type=F32), 0, c)
        cb = dw(jnp.dot(xs, wup_ref[1, c], preferred_element_type=F32), 1, c)
        y_ref[:, c * cf:(c + 1) * cf] = ((ca * jax.nn.sigmoid(ca)) * cb).astype(BF16)

    acc = jnp.dot(y_ref[...], wdn_ref[...], preferred_element_type=F32)
    hn = h_ref[0] + mod_ref[0, 5:6] * acc
    if final_norm:
        hn = _rms(hn, fg_ref[...])
    out_ref[0] = hn


def _ffn(b, h, mod, wup, wdw, bdw, wdn, final_g, tm, final_norm):
    s, l, d = h.shape
    f = wdn.shape[0]
    tile = pl.BlockSpec((1, tm, d), lambda i, t: (i, t, 0))
    return pl.pallas_call(
        functools.partial(_ffn_kernel, final_norm=final_norm),
        out_shape=jax.ShapeDtypeStruct((s, l, d), F32),
        grid=(s, l // tm),
        in_specs=_halo_specs(tm, d, l // tm) + [
            tile,
            pl.BlockSpec((1, 6, d), lambda i, t: (i, 0, 0)),
            _const_spec(wup.shape),
            _const_spec(wdw.shape),
            _const_spec(bdw.shape),
            _const_spec(wdn.shape),
            _const_spec((1, d)),
        ],
        out_specs=tile,
        scratch_shapes=[
            pltpu.VMEM((tm + 2 * HALO, d), BF16),
            pltpu.VMEM((tm, f), BF16),
        ],
        compiler_params=_cparams(2),
        name="conv_ffn",
    )(b, b, b, h, mod, wup, wdw, bdw, wdn, final_g)


def _swap_halves(x):
    lane = lax.broadcasted_iota(jnp.int32, x.shape, 1)
    return jnp.where((lane & (HEAD_DIM - 1)) < HEAD_DIM // 2,
                     pltpu.roll(x, LANES - HEAD_DIM // 2, 1),
                     pltpu.roll(x, HEAD_DIM // 2, 1))


def _qkv_kernel(h_ref, mod_ref, ng_ref, w_ref, cos_ref, sin_ref, q_ref, k_ref, v_ref, *, rope):
    tm = h_ref.shape[1]
    nq = q_ref.shape[2]
    nk = k_ref.shape[2]
    scale = HEAD_DIM ** -0.5 * LOG2E
    for r0 in range(0, tm, SUB_ROWS):
        rows = slice(r0, min(r0 + SUB_ROWS, tm))
        a = _rms_mod(h_ref[0, rows], ng_ref[...], mod_ref[0, 0:1], mod_ref[0, 1:2])
        qkv = jnp.dot(a.astype(BF16), w_ref[...], preferred_element_type=F32)
        if rope:
            cos = cos_ref[rows]
            sin = sin_ref[rows]
            for j in range((nq + nk) // LANES):
                blk = qkv[:, j * LANES:(j + 1) * LANES]
                blk = blk * cos + _swap_halves(blk) * sin
                if j < nq // LANES:
                    q_ref[0, rows, j * LANES:(j + 1) * LANES] = (blk * scale).astype(BF16)
                else:
                    k_ref[0, rows, j * LANES - nq:(j + 1) * LANES - nq] = blk.astype(BF16)
        else:
            q_ref[0, rows] = (qkv[:, :nq] * scale).astype(BF16)
            k_ref[0, rows] = qkv[:, nq:nq + nk].astype(BF16)
        v_ref[0, rows] = qkv[:, nq + nk:].astype(BF16)


def _qkv(h, mod, ng, w_qkv, cos, sin, tm, rope):
    s, l, d = h.shape
    nq = N_HEADS * HEAD_DIM
    nk = N_KV_HEADS * HEAD_DIM
    tile = lambda w: pl.BlockSpec((1, tm, w), lambda i, t: (i, t, 0))
    return pl.pallas_call(
        functools.partial(_qkv_kernel, rope=rope),
        out_shape=(jax.ShapeDtypeStruct((s, l, nq), BF16),
                   jax.ShapeDtypeStruct((s, l, nk), BF16),
                   jax.ShapeDtypeStruct((s, l, nk), BF16)),
        grid=(s, l // tm),
        in_specs=[
            tile(d),
            pl.BlockSpec((1, 6, d), lambda i, t: (i, 0, 0)),
            _const_spec((1, d)),
            _const_spec(w_qkv.shape),
            pl.BlockSpec((tm, LANES), lambda i, t: (t, 0)),
            pl.BlockSpec((tm, LANES), lambda i, t: (t, 0)),
        ],
        out_specs=(tile(nq), tile(nk), tile(nk)),
        compiler_params=_cparams(2),
        name="qkv_rope" if rope else "qkv_ctx",
    )(h, mod, ng, w_qkv, cos, sin)


def _paired_head_order():
    group = N_HEADS // N_KV_HEADS
    order = []
    for c in range(N_KV_HEADS // 2):
        for r in range(group):
            order += [(2 * c) * group + r, (2 * c + 1) * group + r]
    return order


def _attend(q_ref, rows, k_all, v_all, masks, sink_ref, s_ref, p_ref):
    tq = rows.stop - rows.start
    n_keys = k_all.shape[0]
    group = N_HEADS // N_KV_HEADS
    low = lax.broadcasted_iota(jnp.int32, (tq, LANES), 1) < HEAD_DIM
    feat_low = lax.broadcasted_iota(jnp.int32, (LANES, tq), 0) < HEAD_DIM
    lane_hi = lax.broadcasted_iota(jnp.int32, (1, 2 * tq), 1) >= tq
    zero = jnp.zeros((tq, LANES), BF16)
    cols = []
    for c in range(N_KV_HEADS // 2):
        kcol = k_all[:, c * LANES:(c + 1) * LANES]
        vcol = v_all[:, c * LANES:(c + 1) * LANES]
        parts = []
        for r in range(group):
            qcol = q_ref[0, rows, (group * c + r) * LANES:(group * c + r + 1) * LANES]
            parts += [jnp.where(low, qcol, zero), jnp.where(low, zero, qcol)]
        s_ref[c] = lax.dot_general(kcol, jnp.concatenate(parts, axis=0), (((1,), (1,)), ((), ())),
                                   preferred_element_type=F32)
        inv = []
        for r in range(group):
            lanes = slice(r * 2 * tq, (r + 1) * 2 * tq)
            head = 2 * (group * c + r)
            snk = jnp.where(lane_hi, sink_ref[0, head + 1], sink_ref[0, head]) * LOG2E
            blocks = []
            for n in range(n_keys // tq):
                blk = s_ref[c, n * tq:(n + 1) * tq, lanes]
                if n in masks:
                    blk = jnp.where(masks[n], blk, NEG)
                blocks.append(blk)
            m = blocks[0]
            for blk in blocks[1:]:
                m = jnp.maximum(m, blk)
            m = jnp.maximum(jnp.max(m, axis=0, keepdims=True), snk)
            acc = None
            for n, blk in enumerate(blocks):
                e = jnp.exp2(blk - m)
                acc = e if acc is None else acc + e
                p_ref[c, n * tq:(n + 1) * tq, lanes] = e.astype(BF16)
            denom = jnp.sum(acc, axis=0, keepdims=True) + jnp.exp2(snk - m)
            inv.append(1.0 / denom)
        o_t = lax.dot_general(vcol, p_ref[c], (((0,), (0,)), ((), ())), preferred_element_type=F32)
        for r in range(group):
            blk = o_t[:, r * 2 * tq:(r + 1) * 2 * tq] * inv[r]
            cols.append(jnp.where(feat_low, blk[:, 0:tq], blk[:, tq:2 * tq]).T)
    return jnp.concatenate(cols, axis=1)


def _attn_kernel(*refs, has_band, n_blocks):
    if has_band:
        (q_ref, kp_ref, kc_ref, kn_ref, vp_ref, vc_ref, vn_ref, kx_ref, vx_ref, sink_ref,
         h_ref, mod_ref, wo_ref, n2g_ref, hout_ref, b_ref, s_ref, p_ref) = refs
    else:
        (q_ref, kx_ref, vx_ref, sink_ref, h_ref, mod_ref, wo_ref, n2g_ref, hout_ref, b_ref,
         s_ref, p_ref) = refs
    rows_per_step = q_ref.shape[1]

    if has_band:
        tq = BLOCK
        n_sub = rows_per_step // tq
        kb = [kp_ref[0]] + [kc_ref[0, j * tq:(j + 1) * tq] for j in range(n_sub)] + [kn_ref[0]]
        vb = [vp_ref[0]] + [vc_ref[0, j * tq:(j + 1) * tq] for j in range(n_sub)] + [vn_ref[0]]
        key = lax.broadcasted_iota(jnp.int32, (tq, 2 * tq), 0)
        qry = lax.broadcasted_iota(jnp.int32, (tq, 2 * tq), 1)
        qry = jnp.where(qry >= tq, qry - tq, qry)
        outs = []
        for j in range(n_sub):
            blk_idx = pl.program_id(1) * n_sub + j
            masks = {0: (key >= qry) & (blk_idx > 0),
                     2: (key <= qry) & (blk_idx < n_blocks - 1)}
            k_all = jnp.concatenate(kb[j:j + 3] + [kx_ref[0]], axis=0)
            v_all = jnp.concatenate(vb[j:j + 3] + [vx_ref[0]], axis=0)
            outs.append(_attend(q_ref, slice(j * tq, (j + 1) * tq), k_all, v_all, masks, sink_ref,
                                s_ref.at[j], p_ref.at[j]))
        o_all = jnp.concatenate(outs, axis=0)
    else:
        o_all = _attend(q_ref, slice(0, rows_per_step), kx_ref[0], vx_ref[0], {}, sink_ref,
                        s_ref.at[0], p_ref.at[0])
    y = jnp.dot(o_all.astype(BF16), wo_ref[...], preferred_element_type=F32)
    hn = h_ref[0] + mod_ref[0, 2:3] * y
    hout_ref[0] = hn
    b_ref[0] = _rms_mod(hn, n2g_ref[...], mod_ref[0, 3:4], mod_ref[0, 4:5]).astype(BF16)


def _attention(q, k, v, kx, vx, sink, h, mod, w_o, n2g, has_band):
    s, l, d = h.shape
    nq = q.shape[2]
    nk = kx.shape[2]
    n_ctx = kx.shape[1]
    tq = BLOCK if has_band else l
    n_sub = min(ATTN_SUB_BLOCKS, l // tq) if has_band else 1
    rows = n_sub * tq
    n_keys = (3 * tq if has_band else 0) + n_ctx
    tile = lambda w: pl.BlockSpec((1, rows, w), lambda i, t: (i, t, 0))
    prev = pl.BlockSpec((1, tq, nk), lambda i, t: (i, jnp.maximum(t * n_sub - 1, 0), 0))
    nxt = pl.BlockSpec((1, tq, nk), lambda i, t: (i, jnp.minimum((t + 1) * n_sub, l // tq - 1), 0))
    ctx_spec = pl.BlockSpec((1, n_ctx, nk), lambda i, t: (i, 0, 0))
    band_specs = [prev, tile(nk), nxt, prev, tile(nk), nxt] if has_band else []
    band_args = [k, k, k, v, v, v] if has_band else []
    scratch = (n_sub, N_KV_HEADS // 2, n_keys, 2 * (N_HEADS // N_KV_HEADS) * tq)
    return pl.pallas_call(
        functools.partial(_attn_kernel, has_band=has_band, n_blocks=l // tq),
        out_shape=(jax.ShapeDtypeStruct((s, l, d), F32), jax.ShapeDtypeStruct((s, l, d), BF16)),
        grid=(s, l // rows),
        in_specs=[tile(nq)] + band_specs + [
            ctx_spec, ctx_spec,
            pl.BlockSpec(memory_space=pltpu.SMEM),
            tile(d),
            pl.BlockSpec((1, 6, d), lambda i, t: (i, 0, 0)),
            _const_spec(w_o.shape),
            _const_spec((1, d)),
        ],
        out_specs=(tile(d), tile(d)),
        compiler_params=_cparams(2),
        scratch_shapes=[pltpu.VMEM(scratch, F32), pltpu.VMEM(scratch, BF16)],
        name="window_attn" if has_band else "ctx_attn",
    )(q, *band_args, kx, vx, sink, h, mod, w_o, n2g)


def _rope_tables(seq_len):
    rows = seq_len // GRID_W
    row = jnp.broadcast_to(jnp.arange(rows)[:, None], (rows, GRID_W)).reshape(-1).astype(F32)
    col = jnp.broadcast_to(jnp.arange(GRID_W)[None, :], (rows, GRID_W)).reshape(-1).astype(F32)
    n_freq = HEAD_DIM // 4
    inv = ROPE_BASE ** (-jnp.arange(n_freq, dtype=F32) / n_freq)
    ang = jnp.concatenate([row[:, None] * inv, col[:, None] * inv], axis=-1)
    cos, sin = jnp.cos(ang), jnp.sin(ang)
    reps = LANES // HEAD_DIM
    return (jnp.tile(jnp.concatenate([cos, cos], axis=-1), (1, reps)),
            jnp.tile(jnp.concatenate([-sin, sin], axis=-1), (1, reps)))


def _ffn_weights(w_up, w_dw, b_dw, w_down, cf):
    d, two_f = w_up.shape
    f = two_f // 2
    nc = f // cf
    wup = w_up.astype(BF16).reshape(d, 2, nc, cf).transpose(1, 2, 0, 3)
    wdw = w_dw.reshape(w_dw.shape[0], 2, nc, cf).transpose(1, 2, 0, 3)
    bdw = b_dw.reshape(2, nc, 1, cf)
    return wup, wdw, bdw, w_down.astype(BF16)


def kernel(x, c, ctx, c_ctx, ada_w, ada_b, norm1_g, norm2_g, final_g, conv_w_pw1, conv_b_pw1, conv_w_dw,
           conv_b_dw, conv_ln_g, conv_ln_b, conv_w_pw2, conv_b_pw2, attn_w_qkv, attn_w_o, attn_sink,
           ffn_w_up, ffn_w_dw, ffn_b_dw, ffn_w_down):
    bsz, seq, d = x.shape
    n_ctx = ctx.shape[1]
    depth = ada_w.shape[0]
    tm_lat = min(TILE_ROWS, seq)
    tm_conv = min(CONV_TILE_ROWS, seq)
    tm_ctx = min(SUB_ROWS, n_ctx)
    cf = FFN_CHUNK

    rows = -(-(bsz + 1) // 8) * 8
    c_all = jnp.concatenate([c, c_ctx[None, :], jnp.zeros((rows - bsz - 1, d), F32)], axis=0)
    mod_all = _modulation(c_all, ada_w, ada_b)
    cos, sin = _rope_tables(seq)
    final_g2 = final_g.reshape(1, d)

    h_lat, h_ctx = x, ctx
    for i in range(depth):
        last = i == depth - 1
        j = i // 2
        mod_lat = mod_all[i, :bsz].reshape(bsz, 6, d)
        mod_ctx = jnp.broadcast_to(mod_all[i, bsz].reshape(1, 6, d), (bsz, 6, d))
        n1g = norm1_g[i].reshape(1, d)
        n2g = norm2_g[i].reshape(1, d)
        ffn_w = _ffn_weights(ffn_w_up[i], ffn_w_dw[i], ffn_b_dw[i], ffn_w_down[i], cf)
        if i % 2 == 0:
            w1 = conv_w_pw1[j].astype(BF16)
            b1 = conv_b_pw1[j].reshape(1, 2 * d)
            w2 = conv_w_pw2[j].astype(BF16)
            conv_args = (conv_w_dw[j], conv_b_dw[j], conv_ln_g[j].reshape(1, d), conv_ln_b[j].reshape(1, d),
                         w2, conv_b_pw2[j].reshape(1, d), n2g)
            v_lat = _conv_front(h_lat, mod_lat, n1g, w1, b1, tm_lat)
            h_lat, b_lat = _conv_back(v_lat, h_lat, mod_lat, *conv_args, tm_conv)
            if not last:
                v_ctx = _conv_front(h_ctx, mod_ctx, n1g, w1, b1, tm_ctx)
                h_ctx, b_ctx = _conv_back(v_ctx, h_ctx, mod_ctx, *conv_args, tm_ctx)
        else:
            order = jnp.asarray(_paired_head_order())
            q_cols = (order[:, None] * HEAD_DIM + jnp.arange(HEAD_DIM)[None, :]).reshape(-1)
            nq = N_HEADS * HEAD_DIM
            wqkv = jnp.concatenate([attn_w_qkv[j][:, :nq][:, q_cols], attn_w_qkv[j][:, nq:]], axis=1).astype(BF16)
            wo = attn_w_o[j][q_cols, :].astype(BF16)
            sink = attn_sink[j][order].reshape(1, N_HEADS)
            qc, kc, vc = _qkv(h_ctx, mod_ctx, n1g, wqkv, cos[:n_ctx], sin[:n_ctx], tm_ctx, rope=False)
            q, k, v = _qkv(h_lat, mod_lat, n1g, wqkv, cos, sin, tm_lat, rope=True)
            h_lat, b_lat = _attention(q, k, v, kc, vc, sink, h_lat, mod_lat, wo, n2g, has_band=True)
            if not last:
                h_ctx, b_ctx = _attention(qc, None, None, kc, vc, sink, h_ctx, mod_ctx, wo, n2g, has_band=False)
        h_lat = _ffn(b_lat, h_lat, mod_lat, *ffn_w, final_g2, tm_lat, final_norm=last)
        if not last:
            h_ctx = _ffn(b_ctx, h_ctx, mod_ctx, *ffn_w, final_g2, tm_ctx, final_norm=False)
    return h_lat
```

```python
import functools

import jax
import jax.numpy as jnp
from jax import lax
from jax.experimental import pallas as pl
from jax.experimental.pallas import tpu as pltpu

F32 = jnp.float32
BF16 = jnp.bfloat16

EPS = 1e-6
NEG = -1e30
HEAD_DIM = 64
N_HEADS = 16
N_KV_HEADS = 4
WINDOW = 128
BLOCK = 128
GRID_W = 64
ROPE_BASE = 10000.0
LOG2E = 1.4426950408889634

LANES = 128
HALO = 16
VMEM_LIMIT = 56 * 1024 * 1024
MXU_DIM = 256
TILE_ROWS = 1024
CONV_TILE_ROWS = 512
SUB_ROWS = MXU_DIM
FFN_CHUNK = MXU_DIM
ATTN_SUB_BLOCKS = 4


def _cparams(n_axes):
    return pltpu.CompilerParams(
        dimension_semantics=("arbitrary",) * n_axes,
        vmem_limit_bytes=VMEM_LIMIT,
    )


def _const_spec(shape):
    nd = len(shape)
    return pl.BlockSpec(shape, lambda *_: (0,) * nd, pipeline_mode=pl.Buffered(1))


def _rms_mod(x, g, shift, scale):
    ms = jnp.mean(x * x, axis=-1, keepdims=True)
    y = x * lax.rsqrt(ms + EPS) * g
    return y * (1.0 + scale) + shift


def _rms(x, g):
    ms = jnp.mean(x * x, axis=-1, keepdims=True)
    return x * lax.rsqrt(ms + EPS) * g


def _mod_kernel(c_ref, w_ref, b_ref, o_ref):
    c = c_ref[...]
    sc = c * jax.nn.sigmoid(c)
    o_ref[0] = jnp.dot(sc, w_ref[0], preferred_element_type=F32,
                       precision=lax.Precision.HIGHEST) + b_ref[0]


def _modulation(c_all, ada_w, ada_b):
    depth, d, n = ada_w.shape
    rows = c_all.shape[0]
    nb = 1536
    return pl.pallas_call(
        _mod_kernel,
        out_shape=jax.ShapeDtypeStruct((depth, rows, n), F32),
        grid=(depth, n // nb),
        in_specs=[
            pl.BlockSpec((rows, d), lambda i, j: (0, 0)),
            pl.BlockSpec((1, d, nb), lambda i, j: (i, 0, j)),
            pl.BlockSpec((1, 1, nb), lambda i, j: (i, 0, j)),
        ],
        out_specs=pl.BlockSpec((1, rows, nb), lambda i, j: (i, 0, j)),
        compiler_params=_cparams(2),
        name="adaln_mod",
    )(c_all, ada_w, ada_b.reshape(depth, 1, n))


def _conv_front_kernel(h_ref, mod_ref, ng_ref, w_ref, b_ref, v_ref):
    tm, d = h_ref.shape[1:]
    for r0 in range(0, tm, SUB_ROWS):
        rows = slice(r0, min(r0 + SUB_ROWS, tm))
        a = _rms_mod(h_ref[0, rows], ng_ref[...], mod_ref[0, 0:1], mod_ref[0, 1:2])
        u = jnp.dot(a.astype(BF16), w_ref[...], preferred_element_type=F32) + b_ref[...]
        v_ref[0, rows] = u[:, :d] * jax.nn.sigmoid(u[:, d:])


def _conv_front(h, mod, ng, w_pw1, b_pw1, tm):
    s, l, d = h.shape
    return pl.pallas_call(
        _conv_front_kernel,
        out_shape=jax.ShapeDtypeStruct((s, l, d), F32),
        grid=(s, l // tm),
        in_specs=[
            pl.BlockSpec((1, tm, d), lambda i, t: (i, t, 0)),
            pl.BlockSpec((1, 6, d), lambda i, t: (i, 0, 0)),
            _const_spec((1, d)),
            _const_spec((d, 2 * d)),
            _const_spec((1, 2 * d)),
        ],
        out_specs=pl.BlockSpec((1, tm, d), lambda i, t: (i, t, 0)),
        compiler_params=_cparams(2),
        name="conv_front",
    )(h, mod, ng, w_pw1, b_pw1)


def _halo_specs(tm, width, n_tiles):
    r = tm // HALO

    def prev_map(i, t):
        return (i, jnp.maximum(t * r - 1, 0), 0)

    def next_map(i, t):
        return (i, jnp.minimum((t + 1) * r, n_tiles * r - 1), 0)

    return [
        pl.BlockSpec((1, HALO, width), prev_map),
        pl.BlockSpec((1, tm, width), lambda i, t: (i, t, 0)),
        pl.BlockSpec((1, HALO, width), next_map),
    ]


def _conv_back_kernel(vp_ref, v_ref, vn_ref, h_ref, mod_ref, wdw_ref, bdw_ref, lng_ref, lnb_ref,
                      w2_ref, b2_ref, n2g_ref, hout_ref, b_ref, vext_ref, conv_ref, *, n_taps):
    t = pl.program_id(1)
    nt = pl.num_programs(1)
    tm = v_ref.shape[1]
    d = v_ref.shape[2]
    ncb = d // LANES
    rc = 64

    vp = jnp.where(t > 0, vp_ref[0], 0.0)
    vn = jnp.where(t < nt - 1, vn_ref[0], 0.0)
    for cb in range(ncb):
        cs = slice(cb * LANES, (cb + 1) * LANES)
        vext_ref[cb, 0:HALO, :] = vp[:, cs]
        vext_ref[cb, HALO:HALO + tm, :] = v_ref[0, :, cs]
        vext_ref[cb, HALO + tm:, :] = vn[:, cs]

    base = HALO - (n_taps - 1) // 2

    def col_block(cb, carry):
        bias = bdw_ref[cb]
        for rb in range(tm // rc):
            acc = jnp.zeros((rc, LANES), F32) + bias
            for k in range(n_taps):
                off = rb * rc + base + k
                acc = acc + vext_ref[cb, off:off + rc, :] * wdw_ref[cb, k:k + 1, :]
            conv_ref[cb, rb * rc:(rb + 1) * rc, :] = acc
        return carry

    lax.fori_loop(0, ncb, col_block, 0)

    y = jnp.concatenate([conv_ref[cb] for cb in range(ncb)], axis=1)
    mu = jnp.mean(y, axis=-1, keepdims=True)
    yc = y - mu
    var = jnp.mean(yc * yc, axis=-1, keepdims=True)
    z = yc * lax.rsqrt(var + EPS) * lng_ref[...] + lnb_ref[...]
    z = z * jax.nn.sigmoid(z)
    o = jnp.dot(z.astype(BF16), w2_ref[...], preferred_element_type=F32) + b2_ref[...]
    hn = h_ref[0] + mod_ref[0, 2:3] * o
    hout_ref[0] = hn
    b_ref[0] = _rms_mod(hn, n2g_ref[...], mod_ref[0, 3:4], mod_ref[0, 4:5]).astype(BF16)


def _conv_back(v, h, mod, w_dw, b_dw, ln_g, ln_b, w_pw2, b_pw2, n2g, tm):
    s, l, d = h.shape
    n_taps = w_dw.shape[0]
    ncb = d // LANES
    taps_pad = -(-n_taps // 8) * 8
    wdw = jnp.pad(w_dw, ((0, taps_pad - n_taps), (0, 0))).reshape(taps_pad, ncb, LANES).transpose(1, 0, 2)
    bdw = b_dw.reshape(ncb, 1, LANES)
    tile = pl.BlockSpec((1, tm, d), lambda i, t: (i, t, 0))
    return pl.pallas_call(
        functools.partial(_conv_back_kernel, n_taps=n_taps),
        out_shape=(jax.ShapeDtypeStruct((s, l, d), F32), jax.ShapeDtypeStruct((s, l, d), BF16)),
        grid=(s, l // tm),
        in_specs=_halo_specs(tm, d, l // tm) + [
            tile,
            pl.BlockSpec((1, 6, d), lambda i, t: (i, 0, 0)),
            _const_spec((ncb, taps_pad, LANES)),
            _const_spec((ncb, 1, LANES)),
            _const_spec((1, d)),
            _const_spec((1, d)),
            _const_spec((d, d)),
            _const_spec((1, d)),
            _const_spec((1, d)),
        ],
        out_specs=(tile, tile),
        scratch_shapes=[
            pltpu.VMEM((ncb, tm + 2 * HALO, LANES), F32),
            pltpu.VMEM((ncb, tm, LANES), F32),
        ],
        compiler_params=_cparams(2),
        name="conv_back",
    )(v, v, v, h, mod, wdw, bdw, ln_g, ln_b, w_pw2, b_pw2, n2g)


def _ffn_kernel(xp_ref, x_ref, xn_ref, h_ref, mod_ref, wup_ref, wdw_ref, bdw_ref, wdn_ref, fg_ref,
                out_ref, xs_ref, y_ref, u_ref, *, final_norm, cf):
    t = pl.program_id(1)
    nt = pl.num_programs(1)
    tm = x_ref.shape[1]
    f = wdn_ref.shape[0]
    n_chunks = f // cf

    xs_ref[0:HALO, :] = jnp.where(t > 0, xp_ref[0], jnp.zeros_like(xp_ref[0]))
    xs_ref[HALO:HALO + tm, :] = x_ref[0]
    xs_ref[HALO + tm:, :] = jnp.where(t < nt - 1, xn_ref[0], jnp.zeros_like(xn_ref[0]))

    def dw(half, c):
        cols = slice(half * f + c * cf, half * f + (c + 1) * cf)
        u = jnp.dot(xs_ref[...], wup_ref[:, cols], preferred_element_type=F32)
        w = wdw_ref[:, cols]
        b = bdw_ref[:, cols]
        outs = []
        for j in range(cf // LANES):
            slot = (2 * (c % 2) + half) * (cf // LANES) + j
            ls = slice(j * LANES, (j + 1) * LANES)
            u_ref[slot] = u[:, ls]
            outs.append(u_ref[slot, HALO - 1:HALO - 1 + tm, :] * w[0:1, ls]
                        + u[HALO:HALO + tm, ls] * w[1:2, ls]
                        + u_ref[slot, HALO + 1:HALO + 1 + tm, :] * w[2:3, ls]
                        + b[:, ls])
        return outs

    for c in range(n_chunks):
        ca = dw(0, c)
        cb = dw(1, c)
        for j in range(cf // LANES):
            y_ref[:, c * cf + j * LANES:c * cf + (j + 1) * LANES] = (
                (ca[j] * jax.nn.sigmoid(ca[j])) * cb[j]).astype(BF16)

    acc = jnp.dot(y_ref[...], wdn_ref[...], preferred_element_type=F32)
    hn = h_ref[0] + mod_ref[0, 5:6] * acc
    if final_norm:
        hn = _rms(hn, fg_ref[...])
    out_ref[0] = hn


def _ffn(b, h, mod, wup, wdw, bdw, wdn, final_g, tm, final_norm):
    s, l, d = h.shape
    f = wdn.shape[0]
    tile = pl.BlockSpec((1, tm, d), lambda i, t: (i, t, 0))
    return pl.pallas_call(
        functools.partial(_ffn_kernel, final_norm=final_norm, cf=FFN_CHUNK),
        out_shape=jax.ShapeDtypeStruct((s, l, d), F32),
        grid=(s, l // tm),
        in_specs=_halo_specs(tm, d, l // tm) + [
            tile,
            pl.BlockSpec((1, 6, d), lambda i, t: (i, 0, 0)),
            _const_spec(wup.shape),
            _const_spec(wdw.shape),
            _const_spec(bdw.shape),
            _const_spec(wdn.shape),
            _const_spec((1, d)),
        ],
        out_specs=tile,
        scratch_shapes=[
            pltpu.VMEM((tm + 2 * HALO, d), BF16),
            pltpu.VMEM((tm, f), BF16),
            pltpu.VMEM((4 * (FFN_CHUNK // LANES), tm + 2 * HALO, LANES), F32),
        ],
        compiler_params=_cparams(2),
        name="conv_ffn",
    )(b, b, b, h, mod, wup, wdw, bdw, wdn, final_g)


def _swap_halves(x):
    lane = lax.broadcasted_iota(jnp.int32, x.shape, 1)
    return jnp.where((lane & (HEAD_DIM - 1)) < HEAD_DIM // 2,
                     pltpu.roll(x, LANES - HEAD_DIM // 2, 1),
                     pltpu.roll(x, HEAD_DIM // 2, 1))


def _qkv_kernel(h_ref, mod_ref, ng_ref, w_ref, cos_ref, sin_ref, q_ref, k_ref, v_ref, *, rope):
    tm = h_ref.shape[1]
    nq = q_ref.shape[2]
    nk = k_ref.shape[2]
    scale = HEAD_DIM ** -0.5 * LOG2E
    for r0 in range(0, tm, SUB_ROWS):
        rows = slice(r0, min(r0 + SUB_ROWS, tm))
        a = _rms_mod(h_ref[0, rows], ng_ref[...], mod_ref[0, 0:1], mod_ref[0, 1:2])
        qkv = jnp.dot(a.astype(BF16), w_ref[...], preferred_element_type=F32)
        if rope:
            cos = cos_ref[rows]
            sin = sin_ref[rows]
            for j in range((nq + nk) // LANES):
                blk = qkv[:, j * LANES:(j + 1) * LANES]
                blk = blk * cos + _swap_halves(blk) * sin
                if j < nq // LANES:
                    q_ref[0, rows, j * LANES:(j + 1) * LANES] = (blk * scale).astype(BF16)
                else:
                    k_ref[0, rows, j * LANES - nq:(j + 1) * LANES - nq] = blk.astype(BF16)
        else:
            q_ref[0, rows] = (qkv[:, :nq] * scale).astype(BF16)
            k_ref[0, rows] = qkv[:, nq:nq + nk].astype(BF16)
        v_ref[0, rows] = qkv[:, nq + nk:].astype(BF16)


def _qkv(h, mod, ng, w_qkv, cos, sin, tm, rope):
    s, l, d = h.shape
    nq = N_HEADS * HEAD_DIM
    nk = N_KV_HEADS * HEAD_DIM
    tile = lambda w: pl.BlockSpec((1, tm, w), lambda i, t: (i, t, 0))
    return pl.pallas_call(
        functools.partial(_qkv_kernel, rope=rope),
        out_shape=(jax.ShapeDtypeStruct((s, l, nq), BF16),
                   jax.ShapeDtypeStruct((s, l, nk), BF16),
                   jax.ShapeDtypeStruct((s, l, nk), BF16)),
        grid=(s, l // tm),
        in_specs=[
            tile(d),
            pl.BlockSpec((1, 6, d), lambda i, t: (i, 0, 0)),
            _const_spec((1, d)),
            _const_spec(w_qkv.shape),
            pl.BlockSpec((tm, LANES), lambda i, t: (t, 0)),
            pl.BlockSpec((tm, LANES), lambda i, t: (t, 0)),
        ],
        out_specs=(tile(nq), tile(nk), tile(nk)),
        compiler_params=_cparams(2),
        name="qkv_rope" if rope else "qkv_ctx",
    )(h, mod, ng, w_qkv, cos, sin)


def _paired_head_order():
    group = N_HEADS // N_KV_HEADS
    order = []
    for c in range(N_KV_HEADS // 2):
        for r in range(group):
            order += [(2 * c) * group + r, (2 * c + 1) * group + r]
    return order


def _attend(q_ref, rows, k_all, v_all, masks, sink_ref, s_ref, p_ref):
    tq = rows.stop - rows.start
    n_keys = k_all.shape[0]
    group = N_HEADS // N_KV_HEADS
    low = lax.broadcasted_iota(jnp.int32, (tq, LANES), 1) < HEAD_DIM
    feat_low = lax.broadcasted_iota(jnp.int32, (LANES, tq), 0) < HEAD_DIM
    lane_hi = lax.broadcasted_iota(jnp.int32, (1, 2 * tq), 1) >= tq
    zero = jnp.zeros((tq, LANES), BF16)
    cols = []
    for c in range(N_KV_HEADS // 2):
        kcol = k_all[:, c * LANES:(c + 1) * LANES]
        vcol = v_all[:, c * LANES:(c + 1) * LANES]
        parts = []
        for r in range(group):
            qcol = q_ref[0, rows, (group * c + r) * LANES:(group * c + r + 1) * LANES]
            parts += [jnp.where(low, qcol, zero), jnp.where(low, zero, qcol)]
        s_ref[c] = lax.dot_general(kcol, jnp.concatenate(parts, axis=0), (((1,), (1,)), ((), ())),
                                   preferred_element_type=F32)
        inv = []
        for r in range(group):
            lanes = slice(r * 2 * tq, (r + 1) * 2 * tq)
            head = 2 * (group * c + r)
            snk = jnp.where(lane_hi, sink_ref[0, head + 1], sink_ref[0, head]) * LOG2E
            blocks = []
            for n in range(n_keys // tq):
                blk = s_ref[c, n * tq:(n + 1) * tq, lanes]
                if n in masks:
                    blk = jnp.where(masks[n], blk, NEG)
                blocks.append(blk)
            m = blocks[0]
            for blk in blocks[1:]:
                m = jnp.maximum(m, blk)
            m = jnp.maximum(jnp.max(m, axis=0, keepdims=True), snk)
            acc = None
            for n, blk in enumerate(blocks):
                e = jnp.exp2(blk - m)
                acc = e if acc is None else acc + e
                p_ref[c, n * tq:(n + 1) * tq, lanes] = e.astype(BF16)
            denom = jnp.sum(acc, axis=0, keepdims=True) + jnp.exp2(snk - m)
            inv.append(1.0 / denom)
        o_t = lax.dot_general(vcol, p_ref[c], (((0,), (0,)), ((), ())), preferred_element_type=F32)
        for r in range(group):
            blk = o_t[:, r * 2 * tq:(r + 1) * 2 * tq] * inv[r]
            cols.append(jnp.where(feat_low, blk[:, 0:tq], blk[:, tq:2 * tq]).T)
    return jnp.concatenate(cols, axis=1)


def _attn_kernel(*refs, has_band, n_blocks):
    if has_band:
        (q_ref, kp_ref, kc_ref, kn_ref, vp_ref, vc_ref, vn_ref, kx_ref, vx_ref, sink_ref,
         h_ref, mod_ref, wo_ref, n2g_ref, hout_ref, b_ref, s_ref, p_ref) = refs
    else:
        (q_ref, kx_ref, vx_ref, sink_ref, h_ref, mod_ref, wo_ref, n2g_ref, hout_ref, b_ref,
         s_ref, p_ref) = refs
    rows_per_step = q_ref.shape[1]

    if has_band:
        tq = BLOCK
        n_sub = rows_per_step // tq
        kb = [kp_ref[0]] + [kc_ref[0, j * tq:(j + 1) * tq] for j in range(n_sub)] + [kn_ref[0]]
        vb = [vp_ref[0]] + [vc_ref[0, j * tq:(j + 1) * tq] for j in range(n_sub)] + [vn_ref[0]]
        key = lax.broadcasted_iota(jnp.int32, (tq, 2 * tq), 0)
        qry = lax.broadcasted_iota(jnp.int32, (tq, 2 * tq), 1)
        qry = jnp.where(qry >= tq, qry - tq, qry)
        outs = []
        for j in range(n_sub):
            blk_idx = pl.program_id(1) * n_sub + j
            masks = {0: (key >= qry) & (blk_idx > 0),
                     2: (key <= qry) & (blk_idx < n_blocks - 1)}
            k_all = jnp.concatenate(kb[j:j + 3] + [kx_ref[0]], axis=0)
            v_all = jnp.concatenate(vb[j:j + 3] + [vx_ref[0]], axis=0)
            outs.append(_attend(q_ref, slice(j * tq, (j + 1) * tq), k_all, v_all, masks, sink_ref,
                                s_ref.at[j], p_ref.at[j]))
        o_all = jnp.concatenate(outs, axis=0)
    else:
        o_all = _attend(q_ref, slice(0, rows_per_step), kx_ref[0], vx_ref[0], {}, sink_ref,
                        s_ref.at[0], p_ref.at[0])
    y = jnp.dot(o_all.astype(BF16), wo_ref[...], preferred_element_type=F32)
    hn = h_ref[0] + mod_ref[0, 2:3] * y
    hout_ref[0] = hn
    b_ref[0] = _rms_mod(hn, n2g_ref[...], mod_ref[0, 3:4], mod_ref[0, 4:5]).astype(BF16)


def _attention(q, k, v, kx, vx, sink, h, mod, w_o, n2g, has_band):
    s, l, d = h.shape
    nq = q.shape[2]
    nk = kx.shape[2]
    n_ctx = kx.shape[1]
    tq = BLOCK if has_band else l
    n_sub = min(ATTN_SUB_BLOCKS, l // tq) if has_band else 1
    rows = n_sub * tq
    n_keys = (3 * tq if has_band else 0) + n_ctx
    tile = lambda w: pl.BlockSpec((1, rows, w), lambda i, t: (i, t, 0))
    prev = pl.BlockSpec((1, tq, nk), lambda i, t: (i, jnp.maximum(t * n_sub - 1, 0), 0))
    nxt = pl.BlockSpec((1, tq, nk), lambda i, t: (i, jnp.minimum((t + 1) * n_sub, l // tq - 1), 0))
    ctx_spec = pl.BlockSpec((1, n_ctx, nk), lambda i, t: (i, 0, 0))
    band_specs = [prev, tile(nk), nxt, prev, tile(nk), nxt] if has_band else []
    band_args = [k, k, k, v, v, v] if has_band else []
    scratch = (n_sub, N_KV_HEADS // 2, n_keys, 2 * (N_HEADS // N_KV_HEADS) * tq)
    return pl.pallas_call(
        functools.partial(_attn_kernel, has_band=has_band, n_blocks=l // tq),
        out_shape=(jax.ShapeDtypeStruct((s, l, d), F32), jax.ShapeDtypeStruct((s, l, d), BF16)),
        grid=(s, l // rows),
        in_specs=[tile(nq)] + band_specs + [
            ctx_spec, ctx_spec,
            pl.BlockSpec(memory_space=pltpu.SMEM),
            tile(d),
            pl.BlockSpec((1, 6, d), lambda i, t: (i, 0, 0)),
            _const_spec(w_o.shape),
            _const_spec((1, d)),
        ],
        out_specs=(tile(d), tile(d)),
        compiler_params=_cparams(2),
        scratch_shapes=[pltpu.VMEM(scratch, F32), pltpu.VMEM(scratch, BF16)],
        name="window_attn" if has_band else "ctx_attn",
    )(q, *band_args, kx, vx, sink, h, mod, w_o, n2g)


def _rope_tables(seq_len):
    rows = seq_len // GRID_W
    row = jnp.broadcast_to(jnp.arange(rows)[:, None], (rows, GRID_W)).reshape(-1).astype(F32)
    col = jnp.broadcast_to(jnp.arange(GRID_W)[None, :], (rows, GRID_W)).reshape(-1).astype(F32)
    n_freq = HEAD_DIM // 4
    inv = ROPE_BASE ** (-jnp.arange(n_freq, dtype=F32) / n_freq)
    ang = jnp.concatenate([row[:, None] * inv, col[:, None] * inv], axis=-1)
    cos, sin = jnp.cos(ang), jnp.sin(ang)
    reps = LANES // HEAD_DIM
    return (jnp.tile(jnp.concatenate([cos, cos], axis=-1), (1, reps)),
            jnp.tile(jnp.concatenate([-sin, sin], axis=-1), (1, reps)))


def _ffn_weights(w_up, w_dw, b_dw, w_down):
    assert w_down.shape[0] % FFN_CHUNK == 0
    return w_up.astype(BF16), w_dw, b_dw.reshape(1, -1), w_down.astype(BF16)


def kernel(x, c, ctx, c_ctx, ada_w, ada_b, norm1_g, norm2_g, final_g, conv_w_pw1, conv_b_pw1, conv_w_dw,
           conv_b_dw, conv_ln_g, conv_ln_b, conv_w_pw2, conv_b_pw2, attn_w_qkv, attn_w_o, attn_sink,
           ffn_w_up, ffn_w_dw, ffn_b_dw, ffn_w_down):
    bsz, seq, d = x.shape
    n_ctx = ctx.shape[1]
    depth = ada_w.shape[0]
    tm_lat = min(TILE_ROWS, seq)
    tm_conv = min(CONV_TILE_ROWS, seq)
    tm_ctx = min(SUB_ROWS, n_ctx)

    rows = -(-(bsz + 1) // 8) * 8
    c_all = jnp.concatenate([c, c_ctx[None, :], jnp.zeros((rows - bsz - 1, d), F32)], axis=0)
    mod_all = _modulation(c_all, ada_w, ada_b)
    cos, sin = _rope_tables(seq)
    final_g2 = final_g.reshape(1, d)

    h_lat, h_ctx = x, ctx
    for i in range(depth):
        last = i == depth - 1
        j = i // 2
        mod_lat = mod_all[i, :bsz].reshape(bsz, 6, d)
        mod_ctx = jnp.broadcast_to(mod_all[i, bsz].reshape(1, 6, d), (bsz, 6, d))
        n1g = norm1_g[i].reshape(1, d)
        n2g = norm2_g[i].reshape(1, d)
        ffn_w = _ffn_weights(ffn_w_up[i], ffn_w_dw[i], ffn_b_dw[i], ffn_w_down[i])
        if i % 2 == 0:
            w1 = conv_w_pw1[j].astype(BF16)
            b1 = conv_b_pw1[j].reshape(1, 2 * d)
            w2 = conv_w_pw2[j].astype(BF16)
            conv_args = (conv_w_dw[j], conv_b_dw[j], conv_ln_g[j].reshape(1, d), conv_ln_b[j].reshape(1, d),
                         w2, conv_b_pw2[j].reshape(1, d), n2g)
            v_lat = _conv_front(h_lat, mod_lat, n1g, w1, b1, tm_lat)
            h_lat, b_lat = _conv_back(v_lat, h_lat, mod_lat, *conv_args, tm_conv)
            if not last:
                v_ctx = _conv_front(h_ctx, mod_ctx, n1g, w1, b1, tm_ctx)
                h_ctx, b_ctx = _conv_back(v_ctx, h_ctx, mod_ctx, *conv_args, tm_ctx)
        else:
            order = jnp.asarray(_paired_head_order())
            q_cols = (order[:, None] * HEAD_DIM + jnp.arange(HEAD_DIM)[None, :]).reshape(-1)
            nq = N_HEADS * HEAD_DIM
            wqkv = jnp.concatenate([attn_w_qkv[j][:, :nq][:, q_cols], attn_w_qkv[j][:, nq:]], axis=1).astype(BF16)
            wo = attn_w_o[j][q_cols, :].astype(BF16)
            sink = attn_sink[j][order].reshape(1, N_HEADS)
            qc, kc, vc = _qkv(h_ctx, mod_ctx, n1g, wqkv, cos[:n_ctx], sin[:n_ctx], tm_ctx, rope=False)
            q, k, v = _qkv(h_lat, mod_lat, n1g, wqkv, cos, sin, tm_lat, rope=True)
            h_lat, b_lat = _attention(q, k, v, kc, vc, sink, h_lat, mod_lat, wo, n2g, has_band=True)
            if not last:
                h_ctx, b_ctx = _attention(qc, None, None, kc, vc, sink, h_ctx, mod_ctx, wo, n2g, has_band=False)
        h_lat = _ffn(b_lat, h_lat, mod_lat, *ffn_w, final_g2, tm_lat, final_norm=last)
        if not last:
            h_ctx = _ffn(b_ctx, h_ctx, mod_ctx, *ffn_w, final_g2, tm_ctx, final_norm=False)
    return h_lat
```

```python
import functools

import jax
import jax.numpy as jnp
from jax import lax
from jax.experimental import pallas as pl
from jax.experimental.pallas import tpu as pltpu

F32 = jnp.float32
BF16 = jnp.bfloat16

EPS = 1e-6
NEG = -1e30
HEAD_DIM = 64
N_HEADS = 16
N_KV_HEADS = 4
WINDOW = 128
BLOCK = 128
GRID_W = 64
ROPE_BASE = 10000.0
LOG2E = 1.4426950408889634

LANES = 128
HALO = 16
VMEM_LIMIT = 56 * 1024 * 1024
MXU_DIM = 256
TILE_ROWS = 1024
FRONT_TILE_ROWS = 2048
CONV_TILE_ROWS = 1024
SUB_ROWS = MXU_DIM
FFN_CHUNK = MXU_DIM
ATTN_SUB_BLOCKS = 4


def _cparams(n_axes, flags=None):
    return pltpu.CompilerParams(
        dimension_semantics=("arbitrary",) * n_axes,
        vmem_limit_bytes=VMEM_LIMIT,
        flags=flags,
    )


def _const_spec(shape):
    nd = len(shape)
    return pl.BlockSpec(shape, lambda *_: (0,) * nd, pipeline_mode=pl.Buffered(1))


def _rms_mod(x, g, shift, scale):
    ms = jnp.mean(x * x, axis=-1, keepdims=True)
    y = x * lax.rsqrt(ms + EPS) * g
    return y * (1.0 + scale) + shift


def _rms(x, g):
    ms = jnp.mean(x * x, axis=-1, keepdims=True)
    return x * lax.rsqrt(ms + EPS) * g


def _mod_kernel(c_ref, w_ref, b_ref, o_ref):
    c = c_ref[...]
    sc = c * jax.nn.sigmoid(c)
    o_ref[0] = jnp.dot(sc, w_ref[0], preferred_element_type=F32,
                       precision=lax.Precision.HIGHEST) + b_ref[0]


def _modulation(c_all, ada_w, ada_b):
    depth, d, n = ada_w.shape
    rows = c_all.shape[0]
    nb = 1536
    return pl.pallas_call(
        _mod_kernel,
        out_shape=jax.ShapeDtypeStruct((depth, rows, n), F32),
        grid=(depth, n // nb),
        in_specs=[
            pl.BlockSpec((rows, d), lambda i, j: (0, 0)),
            pl.BlockSpec((1, d, nb), lambda i, j: (i, 0, j)),
            pl.BlockSpec((1, 1, nb), lambda i, j: (i, 0, j)),
        ],
        out_specs=pl.BlockSpec((1, rows, nb), lambda i, j: (i, 0, j)),
        compiler_params=_cparams(2),
        name="adaln_mod",
    )(c_all, ada_w, ada_b.reshape(depth, 1, n))


def _conv_front_kernel(h_ref, mod_ref, ng_ref, w_ref, b_ref, v_ref):
    tm, d = h_ref.shape[1:]
    for r0 in range(0, tm, SUB_ROWS):
        rows = slice(r0, min(r0 + SUB_ROWS, tm))
        a = _rms_mod(h_ref[0, rows], ng_ref[...], mod_ref[0, 0:1], mod_ref[0, 1:2])
        u = jnp.dot(a.astype(BF16), w_ref[...], preferred_element_type=F32) + b_ref[...]
        v_ref[0, rows] = u[:, :d] * jax.nn.sigmoid(u[:, d:])


def _conv_front(h, mod, ng, w_pw1, b_pw1, tm):
    s, l, d = h.shape
    return pl.pallas_call(
        _conv_front_kernel,
        out_shape=jax.ShapeDtypeStruct((s, l, d), F32),
        grid=(s, l // tm),
        in_specs=[
            pl.BlockSpec((1, tm, d), lambda i, t: (i, t, 0)),
            pl.BlockSpec((1, 6, d), lambda i, t: (i, 0, 0)),
            _const_spec((1, d)),
            _const_spec((d, 2 * d)),
            _const_spec((1, 2 * d)),
        ],
        out_specs=pl.BlockSpec((1, tm, d), lambda i, t: (i, t, 0)),
        compiler_params=_cparams(2),
        name="conv_front",
    )(h, mod, ng, w_pw1, b_pw1)


def _halo_specs(tm, width, n_tiles):
    r = tm // HALO

    def prev_map(i, t):
        return (i, jnp.maximum(t * r - 1, 0), 0)

    def next_map(i, t):
        return (i, jnp.minimum((t + 1) * r, n_tiles * r - 1), 0)

    return [
        pl.BlockSpec((1, HALO, width), prev_map),
        pl.BlockSpec((1, tm, width), lambda i, t: (i, t, 0)),
        pl.BlockSpec((1, HALO, width), next_map),
    ]


def _conv_back_kernel(vp_ref, v_ref, vn_ref, h_ref, mod_ref, wdw_ref, bdw_ref, lng_ref, lnb_ref,
                      w2_ref, b2_ref, n2g_ref, hout_ref, b_ref, vext_ref, conv_ref, *, n_taps):
    t = pl.program_id(1)
    nt = pl.num_programs(1)
    tm = v_ref.shape[1]
    d = v_ref.shape[2]
    ncb = d // LANES
    rc = 64

    vp = jnp.where(t > 0, vp_ref[0], 0.0)
    vn = jnp.where(t < nt - 1, vn_ref[0], 0.0)
    for cb in range(ncb):
        cs = slice(cb * LANES, (cb + 1) * LANES)
        vext_ref[cb, 0:HALO, :] = vp[:, cs]
        vext_ref[cb, HALO:HALO + tm, :] = v_ref[0, :, cs]
        vext_ref[cb, HALO + tm:, :] = vn[:, cs]

    base = HALO - (n_taps - 1) // 2

    def col_block(cb, carry):
        bias = bdw_ref[cb]
        for rb in range(tm // rc):
            acc = jnp.zeros((rc, LANES), F32) + bias
            for k in range(n_taps):
                off = rb * rc + base + k
                acc = acc + vext_ref[cb, off:off + rc, :] * wdw_ref[cb, k:k + 1, :]
            conv_ref[cb, rb * rc:(rb + 1) * rc, :] = acc
        return carry

    lax.fori_loop(0, ncb, col_block, 0)

    for r0 in range(0, tm, SUB_ROWS):
        rows = slice(r0, min(r0 + SUB_ROWS, tm))
        y = jnp.concatenate([conv_ref[cb, rows] for cb in range(ncb)], axis=1)
        mu = jnp.mean(y, axis=-1, keepdims=True)
        yc = y - mu
        var = jnp.mean(yc * yc, axis=-1, keepdims=True)
        z = yc * lax.rsqrt(var + EPS) * lng_ref[...] + lnb_ref[...]
        z = z * jax.nn.sigmoid(z)
        o = jnp.dot(z.astype(BF16), w2_ref[...], preferred_element_type=F32) + b2_ref[...]
        hn = h_ref[0, rows] + mod_ref[0, 2:3] * o
        hout_ref[0, rows] = hn
        b_ref[0, rows] = _rms_mod(hn, n2g_ref[...], mod_ref[0, 3:4], mod_ref[0, 4:5]).astype(BF16)


def _conv_back(v, h, mod, w_dw, b_dw, ln_g, ln_b, w_pw2, b_pw2, n2g, tm):
    s, l, d = h.shape
    n_taps = w_dw.shape[0]
    ncb = d // LANES
    taps_pad = -(-n_taps // 8) * 8
    wdw = jnp.pad(w_dw, ((0, taps_pad - n_taps), (0, 0))).reshape(taps_pad, ncb, LANES).transpose(1, 0, 2)
    bdw = b_dw.reshape(ncb, 1, LANES)
    tile = pl.BlockSpec((1, tm, d), lambda i, t: (i, t, 0))
    return pl.pallas_call(
        functools.partial(_conv_back_kernel, n_taps=n_taps),
        out_shape=(jax.ShapeDtypeStruct((s, l, d), F32), jax.ShapeDtypeStruct((s, l, d), BF16)),
        grid=(s, l // tm),
        in_specs=_halo_specs(tm, d, l // tm) + [
            tile,
            pl.BlockSpec((1, 6, d), lambda i, t: (i, 0, 0)),
            _const_spec((ncb, taps_pad, LANES)),
            _const_spec((ncb, 1, LANES)),
            _const_spec((1, d)),
            _const_spec((1, d)),
            _const_spec((d, d)),
            _const_spec((1, d)),
            _const_spec((1, d)),
        ],
        out_specs=(tile, tile),
        scratch_shapes=[
            pltpu.VMEM((ncb, tm + 2 * HALO, LANES), F32),
            pltpu.VMEM((ncb, tm, LANES), F32),
        ],
        compiler_params=_cparams(2),
        name="conv_back",
    )(v, v, v, h, mod, wdw, bdw, ln_g, ln_b, w_pw2, b_pw2, n2g)


def _ffn_kernel(xp_ref, x_ref, xn_ref, h_ref, mod_ref, wup_ref, wdw_ref, bdw_ref, wdn_ref, fg_ref,
                out_ref, xs_ref, y_ref, u_ref, *, final_norm, cf):
    t = pl.program_id(1)
    nt = pl.num_programs(1)
    tm = x_ref.shape[1]
    f = wdn_ref.shape[0]
    n_chunks = f // cf

    xs_ref[0:HALO, :] = jnp.where(t > 0, xp_ref[0], jnp.zeros_like(xp_ref[0]))
    xs_ref[HALO:HALO + tm, :] = x_ref[0]
    xs_ref[HALO + tm:, :] = jnp.where(t < nt - 1, xn_ref[0], jnp.zeros_like(xn_ref[0]))

    def dw(half, c):
        cols = slice(half * f + c * cf, half * f + (c + 1) * cf)
        u = jnp.dot(xs_ref[...], wup_ref[:, cols], preferred_element_type=F32)
        w = wdw_ref[:, cols]
        b = bdw_ref[:, cols]
        outs = []
        for j in range(cf // LANES):
            slot = (2 * (c % 2) + half) * (cf // LANES) + j
            ls = slice(j * LANES, (j + 1) * LANES)
            u_ref[slot] = u[:, ls]
            outs.append(u_ref[slot, HALO - 1:HALO - 1 + tm, :] * w[0:1, ls]
                        + u[HALO:HALO + tm, ls] * w[1:2, ls]
                        + u_ref[slot, HALO + 1:HALO + 1 + tm, :] * w[2:3, ls]
                        + b[:, ls])
        return outs

    for c in range(n_chunks):
        ca = dw(0, c)
        cb = dw(1, c)
        for j in range(cf // LANES):
            y_ref[:, c * cf + j * LANES:c * cf + (j + 1) * LANES] = (
                (ca[j] * jax.nn.sigmoid(ca[j])) * cb[j]).astype(BF16)

    acc = jnp.dot(y_ref[...], wdn_ref[...], preferred_element_type=F32)
    hn = h_ref[0] + mod_ref[0, 5:6] * acc
    if final_norm:
        hn = _rms(hn, fg_ref[...])
    out_ref[0] = hn


def _ffn(b, h, mod, wup, wdw, bdw, wdn, final_g, tm, final_norm):
    s, l, d = h.shape
    f = wdn.shape[0]
    tile = pl.BlockSpec((1, tm, d), lambda i, t: (i, t, 0))
    return pl.pallas_call(
        functools.partial(_ffn_kernel, final_norm=final_norm, cf=FFN_CHUNK),
        out_shape=jax.ShapeDtypeStruct((s, l, d), F32),
        grid=(s, l // tm),
        in_specs=_halo_specs(tm, d, l // tm) + [
            tile,
            pl.BlockSpec((1, 6, d), lambda i, t: (i, 0, 0)),
            _const_spec(wup.shape),
            _const_spec(wdw.shape),
            _const_spec(bdw.shape),
            _const_spec(wdn.shape),
            _const_spec((1, d)),
        ],
        out_specs=tile,
        scratch_shapes=[
            pltpu.VMEM((tm + 2 * HALO, d), BF16),
            pltpu.VMEM((tm, f), BF16),
            pltpu.VMEM((4 * (FFN_CHUNK // LANES), tm + 2 * HALO, LANES), F32),
        ],
        compiler_params=_cparams(2),
        name="conv_ffn",
    )(b, b, b, h, mod, wup, wdw, bdw, wdn, final_g)


def _swap_halves(x):
    lane = lax.broadcasted_iota(jnp.int32, x.shape, 1)
    return jnp.where((lane & (HEAD_DIM - 1)) < HEAD_DIM // 2,
                     pltpu.roll(x, LANES - HEAD_DIM // 2, 1),
                     pltpu.roll(x, HEAD_DIM // 2, 1))


def _qkv_kernel(h_ref, mod_ref, ng_ref, w_ref, cos_ref, sin_ref, q_ref, k_ref, v_ref, *, rope):
    tm = h_ref.shape[1]
    nq = q_ref.shape[2]
    nk = k_ref.shape[2]
    scale = HEAD_DIM ** -0.5 * LOG2E
    for r0 in range(0, tm, SUB_ROWS):
        rows = slice(r0, min(r0 + SUB_ROWS, tm))
        a = _rms_mod(h_ref[0, rows], ng_ref[...], mod_ref[0, 0:1], mod_ref[0, 1:2])
        qkv = jnp.dot(a.astype(BF16), w_ref[...], preferred_element_type=F32)
        if rope:
            cos = cos_ref[rows]
            sin = sin_ref[rows]
            for j in range((nq + nk) // LANES):
                blk = qkv[:, j * LANES:(j + 1) * LANES]
                blk = blk * cos + _swap_halves(blk) * sin
                if j < nq // LANES:
                    q_ref[0, rows, j * LANES:(j + 1) * LANES] = (blk * scale).astype(BF16)
                else:
                    k_ref[0, rows, j * LANES - nq:(j + 1) * LANES - nq] = blk.astype(BF16)
        else:
            q_ref[0, rows] = (qkv[:, :nq] * scale).astype(BF16)
            k_ref[0, rows] = qkv[:, nq:nq + nk].astype(BF16)
        v_ref[0, rows] = qkv[:, nq + nk:].astype(BF16)


def _qkv(h, mod, ng, w_qkv, cos, sin, tm, rope):
    s, l, d = h.shape
    nq = N_HEADS * HEAD_DIM
    nk = N_KV_HEADS * HEAD_DIM
    tile = lambda w: pl.BlockSpec((1, tm, w), lambda i, t: (i, t, 0))
    return pl.pallas_call(
        functools.partial(_qkv_kernel, rope=rope),
        out_shape=(jax.ShapeDtypeStruct((s, l, nq), BF16),
                   jax.ShapeDtypeStruct((s, l, nk), BF16),
                   jax.ShapeDtypeStruct((s, l, nk), BF16)),
        grid=(s, l // tm),
        in_specs=[
            tile(d),
            pl.BlockSpec((1, 6, d), lambda i, t: (i, 0, 0)),
            _const_spec((1, d)),
            _const_spec(w_qkv.shape),
            pl.BlockSpec((tm, LANES), lambda i, t: (t, 0)),
            pl.BlockSpec((tm, LANES), lambda i, t: (t, 0)),
        ],
        out_specs=(tile(nq), tile(nk), tile(nk)),
        compiler_params=_cparams(2),
        name="qkv_rope" if rope else "qkv_ctx",
    )(h, mod, ng, w_qkv, cos, sin)


def _paired_head_order():
    group = N_HEADS // N_KV_HEADS
    order = []
    for c in range(N_KV_HEADS // 2):
        for r in range(group):
            order += [(2 * c) * group + r, (2 * c + 1) * group + r]
    return order


def _attend(q_ref, rows, k_all, v_all, masks, sink_ref, s_ref, p_ref):
    tq = rows.stop - rows.start
    n_keys = k_all.shape[0]
    group = N_HEADS // N_KV_HEADS
    low = lax.broadcasted_iota(jnp.int32, (tq, LANES), 1) < HEAD_DIM
    feat_low = lax.broadcasted_iota(jnp.int32, (LANES, tq), 0) < HEAD_DIM
    lane_hi = lax.broadcasted_iota(jnp.int32, (1, 2 * tq), 1) >= tq
    zero = jnp.zeros((tq, LANES), BF16)
    cols = []
    for c in range(N_KV_HEADS // 2):
        kcol = k_all[:, c * LANES:(c + 1) * LANES]
        vcol = v_all[:, c * LANES:(c + 1) * LANES]
        parts = []
        for r in range(group):
            qcol = q_ref[0, rows, (group * c + r) * LANES:(group * c + r + 1) * LANES]
            parts += [jnp.where(low, qcol, zero), jnp.where(low, zero, qcol)]
        s_ref[c] = lax.dot_general(kcol, jnp.concatenate(parts, axis=0), (((1,), (1,)), ((), ())),
                                   preferred_element_type=F32)
        inv = []
        for r in range(group):
            lanes = slice(r * 2 * tq, (r + 1) * 2 * tq)
            head = 2 * (group * c + r)
            snk = jnp.where(lane_hi, sink_ref[0, head + 1], sink_ref[0, head]) * LOG2E
            blocks = []
            for n in range(n_keys // tq):
                blk = s_ref[c, n * tq:(n + 1) * tq, lanes]
                if n in masks:
                    blk = jnp.where(masks[n], blk, NEG)
                blocks.append(blk)
            m = blocks[0]
            for blk in blocks[1:]:
                m = jnp.maximum(m, blk)
            m = jnp.maximum(jnp.max(m, axis=0, keepdims=True), snk)
            acc = None
            for n, blk in enumerate(blocks):
                e = jnp.exp2(blk - m)
                acc = e if acc is None else acc + e
                p_ref[c, n * tq:(n + 1) * tq, lanes] = e.astype(BF16)
            denom = jnp.sum(acc, axis=0, keepdims=True) + jnp.exp2(snk - m)
            inv.append(1.0 / denom)
        o_t = lax.dot_general(vcol, p_ref[c], (((0,), (0,)), ((), ())), preferred_element_type=F32)
        for r in range(group):
            blk = o_t[:, r * 2 * tq:(r + 1) * 2 * tq] * inv[r]
            cols.append(jnp.where(feat_low, blk[:, 0:tq], blk[:, tq:2 * tq]).T)
    return jnp.concatenate(cols, axis=1)


def _attn_kernel(*refs, has_band, n_blocks):
    if has_band:
        (q_ref, kp_ref, kc_ref, kn_ref, vp_ref, vc_ref, vn_ref, kx_ref, vx_ref, sink_ref,
         h_ref, mod_ref, wo_ref, n2g_ref, hout_ref, b_ref, s_ref, p_ref) = refs
    else:
        (q_ref, kx_ref, vx_ref, sink_ref, h_ref, mod_ref, wo_ref, n2g_ref, hout_ref, b_ref,
         s_ref, p_ref) = refs
    rows_per_step = q_ref.shape[1]

    def finish(rows, o):
        y = jnp.dot(o.astype(BF16), wo_ref[...], preferred_element_type=F32)
        hn = h_ref[0, rows] + mod_ref[0, 2:3] * y
        hout_ref[0, rows] = hn
        b_ref[0, rows] = _rms_mod(hn, n2g_ref[...], mod_ref[0, 3:4], mod_ref[0, 4:5]).astype(BF16)

    if has_band:
        tq = BLOCK
        n_sub = rows_per_step // tq
        kb = [kp_ref[0]] + [kc_ref[0, j * tq:(j + 1) * tq] for j in range(n_sub)] + [kn_ref[0]]
        vb = [vp_ref[0]] + [vc_ref[0, j * tq:(j + 1) * tq] for j in range(n_sub)] + [vn_ref[0]]
        key = lax.broadcasted_iota(jnp.int32, (tq, 2 * tq), 0)
        qry = lax.broadcasted_iota(jnp.int32, (tq, 2 * tq), 1)
        qry = jnp.where(qry >= tq, qry - tq, qry)
        outs = []
        for j in range(n_sub):
            blk_idx = pl.program_id(1) * n_sub + j
            masks = {0: (key >= qry) & (blk_idx > 0),
                     2: (key <= qry) & (blk_idx < n_blocks - 1)}
            k_all = jnp.concatenate(kb[j:j + 3] + [kx_ref[0]], axis=0)
            v_all = jnp.concatenate(vb[j:j + 3] + [vx_ref[0]], axis=0)
            outs.append(_attend(q_ref, slice(j * tq, (j + 1) * tq), k_all, v_all, masks, sink_ref,
                                s_ref.at[j], p_ref.at[j]))
        finish(slice(0, rows_per_step), jnp.concatenate(outs, axis=0))
    else:
        rows = slice(0, rows_per_step)
        finish(rows, _attend(q_ref, rows, kx_ref[0], vx_ref[0], {}, sink_ref, s_ref.at[0], p_ref.at[0]))


def _attention(q, k, v, kx, vx, sink, h, mod, w_o, n2g, has_band):
    s, l, d = h.shape
    nq = q.shape[2]
    nk = kx.shape[2]
    n_ctx = kx.shape[1]
    tq = BLOCK if has_band else l
    n_sub = min(ATTN_SUB_BLOCKS, l // tq) if has_band else 1
    rows = n_sub * tq
    n_keys = (3 * tq if has_band else 0) + n_ctx
    tile = lambda w: pl.BlockSpec((1, rows, w), lambda i, t: (i, t, 0))
    prev = pl.BlockSpec((1, tq, nk), lambda i, t: (i, jnp.maximum(t * n_sub - 1, 0), 0))
    nxt = pl.BlockSpec((1, tq, nk), lambda i, t: (i, jnp.minimum((t + 1) * n_sub, l // tq - 1), 0))
    ctx_spec = pl.BlockSpec((1, n_ctx, nk), lambda i, t: (i, 0, 0))
    band_specs = [prev, tile(nk), nxt, prev, tile(nk), nxt] if has_band else []
    band_args = [k, k, k, v, v, v] if has_band else []
    scratch = (n_sub, N_KV_HEADS // 2, n_keys, 2 * (N_HEADS // N_KV_HEADS) * tq)
    return pl.pallas_call(
        functools.partial(_attn_kernel, has_band=has_band, n_blocks=l // tq),
        out_shape=(jax.ShapeDtypeStruct((s, l, d), F32), jax.ShapeDtypeStruct((s, l, d), BF16)),
        grid=(s, l // rows),
        in_specs=[tile(nq)] + band_specs + [
            ctx_spec, ctx_spec,
            pl.BlockSpec(memory_space=pltpu.SMEM),
            tile(d),
            pl.BlockSpec((1, 6, d), lambda i, t: (i, 0, 0)),
            _const_spec(w_o.shape),
            _const_spec((1, d)),
        ],
        out_specs=(tile(d), tile(d)),
        compiler_params=_cparams(2),
        scratch_shapes=[pltpu.VMEM(scratch, F32), pltpu.VMEM(scratch, BF16)],
        name="window_attn" if has_band else "ctx_attn",
    )(q, *band_args, kx, vx, sink, h, mod, w_o, n2g)


def _rope_tables(seq_len):
    rows = seq_len // GRID_W
    row = jnp.broadcast_to(jnp.arange(rows)[:, None], (rows, GRID_W)).reshape(-1).astype(F32)
    col = jnp.broadcast_to(jnp.arange(GRID_W)[None, :], (rows, GRID_W)).reshape(-1).astype(F32)
    n_freq = HEAD_DIM // 4
    inv = ROPE_BASE ** (-jnp.arange(n_freq, dtype=F32) / n_freq)
    ang = jnp.concatenate([row[:, None] * inv, col[:, None] * inv], axis=-1)
    cos, sin = jnp.cos(ang), jnp.sin(ang)
    reps = LANES // HEAD_DIM
    return (jnp.tile(jnp.concatenate([cos, cos], axis=-1), (1, reps)),
            jnp.tile(jnp.concatenate([-sin, sin], axis=-1), (1, reps)))


def _ffn_weights(w_up, w_dw, b_dw, w_down):
    assert w_down.shape[0] % FFN_CHUNK == 0
    return w_up.astype(BF16), w_dw, b_dw.reshape(1, -1), w_down.astype(BF16)


def kernel(x, c, ctx, c_ctx, ada_w, ada_b, norm1_g, norm2_g, final_g, conv_w_pw1, conv_b_pw1, conv_w_dw,
           conv_b_dw, conv_ln_g, conv_ln_b, conv_w_pw2, conv_b_pw2, attn_w_qkv, attn_w_o, attn_sink,
           ffn_w_up, ffn_w_dw, ffn_b_dw, ffn_w_down):
    bsz, seq, d = x.shape
    n_ctx = ctx.shape[1]
    depth = ada_w.shape[0]
    tm_lat = min(TILE_ROWS, seq)
    tm_front = min(FRONT_TILE_ROWS, seq)
    tm_conv = min(CONV_TILE_ROWS, seq)
    tm_ctx = min(SUB_ROWS, n_ctx)

    rows = -(-(bsz + 1) // 8) * 8
    c_all = jnp.concatenate([c, c_ctx[None, :], jnp.zeros((rows - bsz - 1, d), F32)], axis=0)
    mod_all = _modulation(c_all, ada_w, ada_b)
    cos, sin = _rope_tables(seq)
    final_g2 = final_g.reshape(1, d)

    h_lat, h_ctx = x, ctx
    for i in range(depth):
        last = i == depth - 1
        j = i // 2
        mod_lat = mod_all[i, :bsz].reshape(bsz, 6, d)
        mod_ctx = jnp.broadcast_to(mod_all[i, bsz].reshape(1, 6, d), (bsz, 6, d))
        n1g = norm1_g[i].reshape(1, d)
        n2g = norm2_g[i].reshape(1, d)
        ffn_w = _ffn_weights(ffn_w_up[i], ffn_w_dw[i], ffn_b_dw[i], ffn_w_down[i])
        if i % 2 == 0:
            w1 = conv_w_pw1[j].astype(BF16)
            b1 = conv_b_pw1[j].reshape(1, 2 * d)
            w2 = conv_w_pw2[j].astype(BF16)
            conv_args = (conv_w_dw[j], conv_b_dw[j], conv_ln_g[j].reshape(1, d), conv_ln_b[j].reshape(1, d),
                         w2, conv_b_pw2[j].reshape(1, d), n2g)
            v_lat = _conv_front(h_lat, mod_lat, n1g, w1, b1, tm_front)
            h_lat, b_lat = _conv_back(v_lat, h_lat, mod_lat, *conv_args, tm_conv)
            if not last:
                v_ctx = _conv_front(h_ctx, mod_ctx, n1g, w1, b1, tm_ctx)
                h_ctx, b_ctx = _conv_back(v_ctx, h_ctx, mod_ctx, *conv_args, tm_ctx)
        else:
            order = jnp.asarray(_paired_head_order())
            q_cols = (order[:, None] * HEAD_DIM + jnp.arange(HEAD_DIM)[None, :]).reshape(-1)
            nq = N_HEADS * HEAD_DIM
            wqkv = jnp.concatenate([attn_w_qkv[j][:, :nq][:, q_cols], attn_w_qkv[j][:, nq:]], axis=1).astype(BF16)
            wo = attn_w_o[j][q_cols, :].astype(BF16)
            sink = attn_sink[j][order].reshape(1, N_HEADS)
            qc, kc, vc = _qkv(h_ctx, mod_ctx, n1g, wqkv, cos[:n_ctx], sin[:n_ctx], tm_ctx, rope=False)
            q, k, v = _qkv(h_lat, mod_lat, n1g, wqkv, cos, sin, tm_front, rope=True)
            h_lat, b_lat = _attention(q, k, v, kc, vc, sink, h_lat, mod_lat, wo, n2g, has_band=True)
            if not last:
                h_ctx, b_ctx = _attention(qc, None, None, kc, vc, sink, h_ctx, mod_ctx, wo, n2g, has_band=False)
        h_lat = _ffn(b_lat, h_lat, mod_lat, *ffn_w, final_g2, tm_lat, final_norm=last)
        if not last:
            h_ctx = _ffn(b_ctx, h_ctx, mod_ctx, *ffn_w, final_g2, tm_ctx, final_norm=False)
    return h_lat
```
